```python
import functools
import jax, jax.numpy as jnp
from jax import lax
import numpy as np

D_MODEL = 2048
BATCH = 4
SEQ = 2048
DEPTH = 1
DEC_BATCH = 32
DEC_SEQ = 1
PAST_LEN = 16384
PAGE_SIZE = 128

HEAD_DIM = 128
H_FOX = D_MODEL // (2 * HEAD_DIM)
H_DSA = D_MODEL // (2 * HEAD_DIM)
KV_DSA = 2
H_IDX = 16
D_IDX = 64
TOPK_MAX = 256
N_GROUPS = 4
EXPERTS_PER_GROUP = 8
N_EXPERTS = N_GROUPS * EXPERTS_PER_GROUP
TOP_IN_GROUP = 2
D_EXPERT = 512
ROPE_THETA = 10000.0
Q_BLOCK = 128
EPS = 1e-6
FORGET_BIAS_INIT = 4.0
D_MIX = (H_FOX + H_DSA) * HEAD_DIM
SPLIT_SIZES = (H_FOX * HEAD_DIM, H_FOX * HEAD_DIM, H_FOX * HEAD_DIM, H_FOX,
               H_DSA * HEAD_DIM, KV_DSA * HEAD_DIM, KV_DSA * HEAD_DIM,
               H_IDX * D_IDX, D_IDX, H_IDX)
D_IN_PROJ = sum(SPLIT_SIZES)

kernel_name = 'hymba_fox_dsa_hmoe_step'


def _rms(x):
    xf = x.astype(jnp.float32)
    return (xf * lax.rsqrt(jnp.mean(xf * xf, axis=-1, keepdims=True) + EPS)).astype(x.dtype)


def _rope(x, pos):
    half = x.shape[-1] // 2
    inv = ROPE_THETA ** (-jnp.arange(half, dtype=jnp.float32) / half)
    ang = pos.astype(jnp.float32)[:, None] * inv[None, :]
    cos = jnp.cos(ang)[None, :, None, :]
    sin = jnp.sin(ang)[None, :, None, :]
    xf = x.astype(jnp.float32)
    x1, x2 = xf[..., :half], xf[..., half:]
    return jnp.concatenate([x1 * cos - x2 * sin, x2 * cos + x1 * sin], axis=-1).astype(x.dtype)


def _adaln(c, w_ada, b_ada):
    m = jax.nn.silu(c) @ w_ada + b_ada
    return jnp.split(m[:, None, :], 6, axis=-1)


def _modulate(x, shift, scale):
    return _rms(x) * (1.0 + scale) + shift


def _project(h, pos, w_in, b_forget, q_gain_fox, k_gain_fox, q_gain_dsa, k_gain_dsa):
    B, S, _ = h.shape
    p = h @ w_in
    parts, start = [], 0
    for size in SPLIT_SIZES:
        parts.append(p[..., start:start + size])
        start += size
    q_f, k_f, v_f, f_logit, q_d, k_d, v_d, q_i, k_i, w_i = parts

    def _heads(t, n):
        return t.reshape(B, S, n, HEAD_DIM)

    q_f = _rms(_heads(q_f, H_FOX)) * q_gain_fox
    k_f = _rms(_heads(k_f, H_FOX)) * k_gain_fox
    v_f = _heads(v_f, H_FOX)
    log_f = jax.nn.log_sigmoid((f_logit + b_forget).astype(jnp.float32))
    q_d = _rope(_rms(_heads(q_d, H_DSA)) * q_gain_dsa, pos)
    k_d = _rope(_rms(_heads(k_d, KV_DSA)) * k_gain_dsa, pos)
    v_d = _heads(v_d, KV_DSA)
    q_i = _rope(q_i.reshape(B, S, H_IDX, D_IDX), pos)
    k_i = _rope(k_i[:, :, None, :], pos)[:, :, 0, :]
    w_i = w_i * (H_IDX ** -0.5)
    return q_f, k_f, v_f, log_f, q_d, k_d, v_d, q_i, k_i, w_i


def _index_scores(q_i, k_i, w_i):
    dots = jnp.einsum('bqhd,bkd->bqhk', q_i, k_i).astype(jnp.float32) * (D_IDX ** -0.5)
    return jnp.einsum('bqh,bqhk->bqk', w_i.astype(jnp.float32), jax.nn.relu(dots))


def _sparse_attend(q, k_sel, v_sel, valid):
    B, Q, H, Dh = q.shape
    qg = q.reshape(B, Q, KV_DSA, H // KV_DSA, Dh)
    logits = jnp.einsum('bqgrd,bqngd->bqgrn', qg, k_sel).astype(jnp.float32) * (Dh ** -0.5)
    logits = jnp.where(valid[:, :, None, None, :], logits, -jnp.inf)
    p = jax.nn.softmax(logits, axis=-1).astype(v_sel.dtype)
    return jnp.einsum('bqgrn,bqngd->bqgrd', p, v_sel).reshape(B, Q, H * Dh)


def _fox_prompt(q, k, v, log_f):
    B, S, H, Dh = q.shape
    cum = jnp.cumsum(log_f, axis=1).transpose(0, 2, 1)
    kpos = jnp.arange(S)

    def block(i):
        start = i * Q_BLOCK
        qb = lax.dynamic_slice_in_dim(q, start, Q_BLOCK, axis=1)
        cq = lax.dynamic_slice_in_dim(cum, start, Q_BLOCK, axis=2)
        qpos = start + jnp.arange(Q_BLOCK)
        logits = jnp.einsum('bqhd,bkhd->bhqk', qb, k).astype(jnp.float32) * (Dh ** -0.5)
        logits = logits + cq[..., :, None] - cum[..., None, :]
        logits = jnp.where(kpos[None, :] <= qpos[:, None], logits, -jnp.inf)
        p = jax.nn.softmax(logits, axis=-1).astype(v.dtype)
        return jnp.einsum('bhqk,bkhd->bqhd', p, v).reshape(B, Q_BLOCK, H * Dh)

    out = lax.map(block, jnp.arange(S // Q_BLOCK))
    return out.transpose(1, 0, 2, 3).reshape(B, S, H * Dh)


def _dsa_prompt(q, k, v, q_i, k_i, w_i):
    B, S = q.shape[:2]
    n_sel = min(TOPK_MAX, S // 4)
    kpos = jnp.arange(S)
    b_idx = jnp.arange(B)[:, None, None]

    def block(i):
        start = i * Q_BLOCK

        def sl(t):
            return lax.dynamic_slice_in_dim(t, start, Q_BLOCK, axis=1)

        qpos = start + jnp.arange(Q_BLOCK)
        score = _index_scores(sl(q_i), k_i, sl(w_i))
        score = jnp.where(kpos[None, None, :] <= qpos[None, :, None], score, -jnp.inf)
        _, sel = lax.top_k(score, n_sel)
        valid = sel <= qpos[None, :, None]
        return _sparse_attend(sl(q), k[b_idx, sel], v[b_idx, sel], valid)

    out = lax.map(block, jnp.arange(S // Q_BLOCK))
    return out.transpose(1, 0, 2, 3).reshape(B, S, -1)


def _mix_prompt(q_f, k_f, v_f, log_f, q_d, k_d, v_d, q_i, k_i, w_i):
    return jnp.concatenate([_fox_prompt(q_f, k_f, v_f, log_f),
                            _dsa_prompt(q_d, k_d, v_d, q_i, k_i, w_i)], axis=-1)


def _fox_sample(q, k_new, v_new, log_f_new, cache_k, cache_v, cache_logf, page_table, layer):
    DB, Q, H, Dh = q.shape
    k_past = cache_k[layer, page_table].reshape(DB, PAST_LEN, H, Dh)
    v_past = cache_v[layer, page_table].reshape(DB, PAST_LEN, H, Dh)
    lf_past = cache_logf[layer, page_table].reshape(DB, PAST_LEN, H).astype(jnp.float32)
    cum = jnp.cumsum(jnp.concatenate([lf_past, log_f_new], axis=1), axis=1).transpose(0, 2, 1)
    cq = cum[:, :, PAST_LEN:]
    logits = jnp.concatenate([jnp.einsum('bqhd,bkhd->bhqk', q, k_past),
                              jnp.einsum('bqhd,bkhd->bhqk', q, k_new)], axis=-1).astype(jnp.float32)
    logits = logits * (Dh ** -0.5) + cq[..., :, None] - cum[..., None, :]
    qpos = PAST_LEN + jnp.arange(Q)
    kpos = jnp.arange(PAST_LEN + Q)
    logits = jnp.where(kpos[None, :] <= qpos[:, None], logits, -jnp.inf)
    p = jax.nn.softmax(logits, axis=-1).astype(v_new.dtype)
    o = (jnp.einsum('bhqk,bkhd->bqhd', p[..., :PAST_LEN], v_past)
         + jnp.einsum('bhqk,bkhd->bqhd', p[..., PAST_LEN:], v_new))
    return o.reshape(DB, Q, H * Dh)


def _gather_rows(sel, rows_new, cache, page_table, layer):
    b_idx = jnp.arange(sel.shape[0])[:, None, None]
    s_past = jnp.minimum(sel, PAST_LEN - 1)
    phys = page_table[b_idx, s_past // PAGE_SIZE]
    from_cache = cache[layer, phys, s_past % PAGE_SIZE]
    from_new = rows_new[b_idx, jnp.clip(sel - PAST_LEN, 0, rows_new.shape[1] - 1)]
    return jnp.where((sel < PAST_LEN)[..., None, None], from_cache, from_new)


def _dsa_sample(q, k_new, v_new, q_i, k_i_new, w_i, cache_k, cache_v, cache_idx, page_table, layer):
    DB, Q = q.shape[:2]
    L = PAST_LEN + Q
    n_sel = min(TOPK_MAX, L // 4)
    ki_past = cache_idx[layer, page_table].reshape(DB, PAST_LEN, D_IDX)
    score = _index_scores(q_i, jnp.concatenate([ki_past, k_i_new], axis=1), w_i)
    qpos = PAST_LEN + jnp.arange(Q)
    score = jnp.where(jnp.arange(L)[None, None, :] <= qpos[None, :, None], score, -jnp.inf)
    _, sel = lax.top_k(score, n_sel)
    valid = sel <= qpos[None, :, None]
    k_sel = _gather_rows(sel, k_new, cache_k, page_table, layer)
    v_sel = _gather_rows(sel, v_new, cache_v, page_table, layer)
    return _sparse_attend(q, k_sel, v_sel, valid)


def _mix_sample(q_f, k_f, v_f, log_f, q_d, k_d, v_d, q_i, k_i, w_i, *, layer, page_table,
                cache_fox_k, cache_fox_v, cache_fox_logf, cache_dsa_k, cache_dsa_v, cache_idx_k):
    o_fox = _fox_sample(q_f, k_f, v_f, log_f, cache_fox_k, cache_fox_v, cache_fox_logf, page_table, layer)
    o_dsa = _dsa_sample(q_d, k_d, v_d, q_i, k_i, w_i, cache_dsa_k, cache_dsa_v, cache_idx_k, page_table, layer)
    return jnp.concatenate([o_fox, o_dsa], axis=-1)


def _hier_moe(h, w_router_group, w_router_expert, w_gate, w_up, w_down):
    logit_g = (h @ w_router_group).astype(jnp.float32)
    p_g = jax.nn.softmax(logit_g, axis=-1)
    g_sel = jnp.argmax(logit_g, axis=-1)
    p_gsel = jnp.take_along_axis(p_g, g_sel[:, None], axis=-1)
    logit_e = jnp.einsum('td,gde->tge', h, w_router_expert).astype(jnp.float32)
    logit_e = jnp.take_along_axis(logit_e, g_sel[:, None, None], axis=1)[:, 0]
    top_p, top_i = lax.top_k(jax.nn.softmax(logit_e, axis=-1), TOP_IN_GROUP)
    weight = top_p / jnp.sum(top_p, axis=-1, keepdims=True) * p_gsel
    expert_id = g_sel[:, None] * EXPERTS_PER_GROUP + top_i
    gates = jnp.einsum('tk,tke->te', weight,
                       jax.nn.one_hot(expert_id, N_EXPERTS, dtype=jnp.float32)).astype(h.dtype)
    a = jnp.einsum('td,edf->tef', h, w_gate)
    u = jnp.einsum('td,edf->tef', h, w_up)
    return jnp.einsum('tef,efd->td', jax.nn.silu(a) * u * gates[:, :, None], w_down)


def _trunk_layer(x, c, pos, mix, w_in, b_forget, q_gain_fox, k_gain_fox, q_gain_dsa, k_gain_dsa,
                 w_out, w_ada, b_ada, w_router_group, w_router_expert, w_gate, w_up, w_down):
    sh_a, sc_a, g_a, sh_m, sc_m, g_m = _adaln(c, w_ada, b_ada)
    proj = _project(_modulate(x, sh_a, sc_a), pos, w_in, b_forget,
                    q_gain_fox, k_gain_fox, q_gain_dsa, k_gain_dsa)
    x = x + g_a * (mix(*proj) @ w_out)
    B, S, D = x.shape
    h = _modulate(x, sh_m, sc_m).reshape(B * S, D)
    x = x + g_m * _hier_moe(h, w_router_group, w_router_expert, w_gate, w_up, w_down).reshape(B, S, D)
    q_f, k_f, v_f, log_f, q_d, k_d, v_d, q_i, k_i, w_i = proj
    return x, (k_f, v_f, log_f, k_d, v_d, k_i)


def setup_inputs(seed: int = 0) -> dict:
    key = jax.random.key(seed)
    ks = jax.random.split(key, 28)

    def nrm(i, shape, scale):
        return jax.random.normal(ks[i], shape, jnp.float32) * scale

    n_pages = PAST_LEN // PAGE_SIZE
    n_phys = (DEC_BATCH * n_pages * 5) // 4
    page_table = jax.random.permutation(ks[0], n_phys)[: DEC_BATCH * n_pages]
    page_table = page_table.reshape(DEC_BATCH, n_pages).astype(jnp.int32)
    pool = (DEPTH, n_phys, PAGE_SIZE)
    return {
        'x_prompt': nrm(1, (BATCH, SEQ, D_MODEL), 1.0),
        'x_sample': nrm(2, (DEC_BATCH, DEC_SEQ, D_MODEL), 1.0),
        'cache_fox_k': nrm(3, pool + (H_FOX, HEAD_DIM), 1.0),
        'cache_fox_v': nrm(4, pool + (H_FOX, HEAD_DIM), 1.0),
        'cache_fox_logf': jax.nn.log_sigmoid(FORGET_BIAS_INIT + nrm(5, pool + (H_FOX,), 1.0)),
        'cache_dsa_k': nrm(6, pool + (KV_DSA, HEAD_DIM), 1.0),
        'cache_dsa_v': nrm(7, pool + (KV_DSA, HEAD_DIM), 1.0),
        'cache_idx_k': nrm(8, pool + (D_IDX,), 1.0),
        'page_table': page_table,
        'c_prompt': nrm(9, (BATCH, D_MODEL), 1.0),
        'c_sample': nrm(10, (DEC_BATCH, D_MODEL), 1.0),
        'w_in': nrm(11, (DEPTH, D_MODEL, D_IN_PROJ), D_MODEL ** -0.5),
        'b_forget': FORGET_BIAS_INIT + nrm(12, (DEPTH, H_FOX), 0.5),
        'q_gain_fox': 1.0 + nrm(13, (DEPTH, HEAD_DIM), 0.05),
        'k_gain_fox': 1.0 + nrm(14, (DEPTH, HEAD_DIM), 0.05),
        'q_gain_dsa': 1.0 + nrm(15, (DEPTH, HEAD_DIM), 0.05),
        'k_gain_dsa': 1.0 + nrm(16, (DEPTH, HEAD_DIM), 0.05),
        'w_out': nrm(17, (DEPTH, D_MIX, D_MODEL), D_MIX ** -0.5),
        'w_ada': nrm(18, (DEPTH, D_MODEL, 6 * D_MODEL), 0.5 * D_MODEL ** -0.5),
        'b_ada': nrm(19, (DEPTH, 6 * D_MODEL), 0.02),
        'w_router_group': nrm(20, (DEPTH, D_MODEL, N_GROUPS), D_MODEL ** -0.5),
        'w_router_expert': nrm(21, (DEPTH, N_GROUPS, D_MODEL, EXPERTS_PER_GROUP), D_MODEL ** -0.5),
        'w_gate': nrm(22, (DEPTH, N_EXPERTS, D_MODEL, D_EXPERT), D_MODEL ** -0.5),
        'w_up': nrm(23, (DEPTH, N_EXPERTS, D_MODEL, D_EXPERT), D_MODEL ** -0.5),
        'w_down': nrm(24, (DEPTH, N_EXPERTS, D_EXPERT, D_MODEL), D_EXPERT ** -0.5),
    }


def reference(x_prompt, x_sample, cache_fox_k, cache_fox_v, cache_fox_logf, cache_dsa_k,
              cache_dsa_v, cache_idx_k, page_table, c_prompt, c_sample, w_in, b_forget,
              q_gain_fox, k_gain_fox, q_gain_dsa, k_gain_dsa, w_out, w_ada, b_ada,
              w_router_group, w_router_expert, w_gate, w_up, w_down):
    pos_prompt = jnp.arange(x_prompt.shape[1], dtype=jnp.int32)
    pos_sample = PAST_LEN + jnp.arange(x_sample.shape[1], dtype=jnp.int32)
    y_prompt, y_sample = x_prompt, x_sample
    rows_p, rows_s = [], []
    for l in range(DEPTH):
        lw = (w_in[l], b_forget[l], q_gain_fox[l], k_gain_fox[l], q_gain_dsa[l], k_gain_dsa[l],
              w_out[l], w_ada[l], b_ada[l], w_router_group[l], w_router_expert[l],
              w_gate[l], w_up[l], w_down[l])
        y_prompt, r_p = _trunk_layer(y_prompt, c_prompt, pos_prompt, _mix_prompt, *lw)
        rows_p.append(r_p)
        mix_s = functools.partial(_mix_sample, layer=l, page_table=page_table,
                                  cache_fox_k=cache_fox_k, cache_fox_v=cache_fox_v,
                                  cache_fox_logf=cache_fox_logf, cache_dsa_k=cache_dsa_k,
                                  cache_dsa_v=cache_dsa_v, cache_idx_k=cache_idx_k)
        y_sample, r_s = _trunk_layer(y_sample, c_sample, pos_sample, mix_s, *lw)
        rows_s.append(r_s)
    fk_p, fv_p, fl_p, dk_p, dv_p, ik_p = [jnp.stack(t) for t in zip(*rows_p)]
    fk_s, fv_s, fl_s, dk_s, dv_s, ik_s = [jnp.stack(t) for t in zip(*rows_s)]
    return (y_prompt, y_sample, fk_p, fv_p, fl_p, dk_p, dv_p, ik_p, fk_s, fv_s, fl_s, dk_s, dv_s, ik_s)
```

```python
import functools

import jax
import jax.numpy as jnp
from jax import lax
from jax.experimental import pallas as pl
from jax.experimental.pallas import tpu as pltpu

F32 = jnp.float32
BF16 = jnp.bfloat16
I32 = jnp.int32

HEAD_DIM = 128
D_IDX = 64
H_IDX = 16
KV_DSA = 2
TOPK_MAX = 256
N_GROUPS = 4
EXPERTS_PER_GROUP = 8
N_EXPERTS = N_GROUPS * EXPERTS_PER_GROUP
ROPE_THETA = 10000.0
EPS = 1e-6
PAGE_SIZE = 128

LANES = 128
NEG = -1e30
KEY_NEG_INF = -2139095041
VMEM_LIMIT = 52 * 1024 * 1024

MISC_KI = 0
MISC_F = 64
MISC_W = 72
MISC_END = 88


def _cp(sem):
    return pltpu.CompilerParams(dimension_semantics=sem, vmem_limit_bytes=VMEM_LIMIT)


def _dot(a, b):
    return jnp.dot(a, b, preferred_element_type=F32)


def _dot_nt(a, b):
    return lax.dot_general(a, b, (((1,), (1,)), ((), ())), preferred_element_type=F32)


def _adaln_kernel(c_ref, w_ref, b_ref, o_ref):
    c = c_ref[...]
    s = c / (1.0 + jnp.exp(-c))
    o_ref[...] = _dot(s.astype(BF16), w_ref[...].astype(BF16)) + b_ref[...]


def _adaln(c_all, w_ada, b_ada):
    rows, d = c_all.shape
    n = w_ada.shape[1]
    tn = 1024
    return pl.pallas_call(
        _adaln_kernel,
        grid=(n // tn,),
        in_specs=[pl.BlockSpec((rows, d), lambda j: (0, 0)),
                  pl.BlockSpec((d, tn), lambda j: (0, j)),
                  pl.BlockSpec((1, tn), lambda j: (0, j))],
        out_specs=pl.BlockSpec((rows, tn), lambda j: (0, j)),
        out_shape=jax.ShapeDtypeStruct((rows, n), F32),
        compiler_params=_cp(("parallel",)),
        name="adaln",
    )(c_all, w_ada, b_ada.reshape(1, n))


def _modulated(x, sc_ref, sh_ref):
    ms = jnp.mean(x * x, axis=-1, keepdims=True)
    return (x * lax.rsqrt(ms + EPS)) * (1.0 + sc_ref[...]) + sh_ref[...]


def _rms_heads(acc, gain, n_heads):
    outs = []
    for hh in range(n_heads):
        blk = acc[:, hh * HEAD_DIM:(hh + 1) * HEAD_DIM]
        ms = jnp.mean(blk * blk, axis=-1, keepdims=True)
        outs.append(blk * lax.rsqrt(ms + EPS) * gain)
    return outs


def _rope128(y, cos, sin_signed):
    return y * cos + pltpu.roll(y, HEAD_DIM // 2, axis=1) * sin_signed


def _rope64(y, cos, sin_signed, first_half):
    swapped = jnp.where(first_half, pltpu.roll(y, LANES - D_IDX // 2, axis=1),
                        pltpu.roll(y, D_IDX // 2, axis=1))
    return y * cos + swapped * sin_signed


def _proj_fox_kernel(x_ref, sc_ref, sh_ref, wq_ref, wk_ref, wv_ref, gq_ref, gk_ref,
                     qb_ref, kb_ref, vb_ref, k_ref, v_ref):
    h = _modulated(x_ref[...], sc_ref, sh_ref).astype(BF16)
    q = _rms_heads(_dot(h, wq_ref[...]), gq_ref[...], wq_ref.shape[1] // HEAD_DIM)
    for hh, blk in enumerate(q):
        qb_ref[:, hh * HEAD_DIM:(hh + 1) * HEAD_DIM] = blk.astype(BF16)
    k = _rms_heads(_dot(h, wk_ref[...]), gk_ref[...], wk_ref.shape[1] // HEAD_DIM)
    for hh, blk in enumerate(k):
        sl = slice(hh * HEAD_DIM, (hh + 1) * HEAD_DIM)
        k_ref[:, sl] = blk
        kb_ref[:, sl] = blk.astype(BF16)
    v = _dot(h, wv_ref[...])
    v_ref[...] = v
    vb_ref[...] = v.astype(BF16)


def _proj_dsa_kernel(x_ref, sc_ref, sh_ref, wq_ref, wk_ref, wv_ref, wi_ref, wm_ref,
                     gq_ref, gk_ref, bf_ref, c128_ref, s128_ref, c64_ref, s64_ref,
                     qb_ref, kb_ref, vb_ref, qi_ref, kia_ref, kib_ref, k_ref, v_ref, misc_ref):
    h = _modulated(x_ref[...], sc_ref, sh_ref).astype(BF16)
    c128, s128 = c128_ref[...], s128_ref[...]
    c64, s64 = c64_ref[...], s64_ref[...]
    lane = lax.broadcasted_iota(I32, c64.shape, 1)
    first_half = (lane % D_IDX) < (D_IDX // 2)

    q = _rms_heads(_dot(h, wq_ref[...]), gq_ref[...], wq_ref.shape[1] // HEAD_DIM)
    for hh, blk in enumerate(q):
        qb_ref[:, hh * HEAD_DIM:(hh + 1) * HEAD_DIM] = _rope128(blk, c128, s128).astype(BF16)
    k = _rms_heads(_dot(h, wk_ref[...]), gk_ref[...], wk_ref.shape[1] // HEAD_DIM)
    for hh, blk in enumerate(k):
        sl = slice(hh * HEAD_DIM, (hh + 1) * HEAD_DIM)
        r = _rope128(blk, c128, s128)
        k_ref[:, sl] = r
        kb_ref[:, sl] = r.astype(BF16)
    v = _dot(h, wv_ref[...])
    v_ref[...] = v
    vb_ref[...] = v.astype(BF16)

    qi = _dot(h, wi_ref[...])
    for p in range(wi_ref.shape[1] // LANES):
        sl = slice(p * LANES, (p + 1) * LANES)
        qi_ref[:, sl] = _rope64(qi[:, sl], c64, s64, first_half).astype(BF16)

    m = _dot(h, wm_ref[...])
    roped = _rope64(m, c64, s64, first_half)
    xf = m + bf_ref[...]
    logsig = jnp.minimum(xf, 0.0) - jnp.log1p(jnp.exp(-jnp.abs(xf)))
    wsc = m * (H_IDX ** -0.5)
    ki_only = jnp.where(lane < MISC_F, roped, 0.0)
    misc_ref[...] = jnp.where(lane < MISC_F, roped,
                              jnp.where(lane < MISC_W, logsig,
                                        jnp.where(lane < MISC_END, wsc, 0.0)))
    kia_ref[...] = ki_only.astype(BF16)
    kib_ref[...] = pltpu.roll(ki_only, D_IDX, axis=1).astype(BF16)


def _row_specs(tm, d, rows_per_group, mod_rows):
    tiles_per_group = rows_per_group // tm
    x_spec = pl.BlockSpec((tm, d), lambda i: (i, 0))
    mod_spec = pl.BlockSpec((None, mod_rows, d), lambda i: (i // tiles_per_group, 0, 0))
    return x_spec, mod_spec


def _full(shape):
    nd = len(shape)
    return pl.BlockSpec(shape, lambda i: (0,) * nd)


def _proj_fox(x2d, sc, sh, wq, wk, wv, gq, gk, tm, rows_per_group):
    m, d = x2d.shape
    x_spec, mod_spec = _row_specs(tm, d, rows_per_group, sc.shape[1])
    n = wq.shape[1]
    o_spec = pl.BlockSpec((tm, n), lambda i: (i, 0))
    return pl.pallas_call(
        _proj_fox_kernel,
        grid=(m // tm,),
        in_specs=[x_spec, mod_spec, mod_spec, _full(wq.shape), _full(wk.shape), _full(wv.shape),
                  _full(gq.shape), _full(gk.shape)],
        out_specs=[o_spec] * 5,
        out_shape=[jax.ShapeDtypeStruct((m, n), BF16)] * 3 + [jax.ShapeDtypeStruct((m, n), F32)] * 2,
        compiler_params=_cp(("parallel",)),
        name="proj_fox",
    )(x2d, sc, sh, wq, wk, wv, gq, gk)


def _proj_dsa(x2d, sc, sh, wq, wk, wv, wi, wm, gq, gk, bf, tabs, tm, rows_per_group, tiles_per_seq):
    m, d = x2d.shape
    x_spec, mod_spec = _row_specs(tm, d, rows_per_group, sc.shape[1])
    tab_spec = pl.BlockSpec((tm, LANES), lambda i: (i % tiles_per_seq, 0))
    nq, nk, ni = wq.shape[1], wk.shape[1], wi.shape[1]

    def ospec(n):
        return pl.BlockSpec((tm, n), lambda i: (i, 0))

    def oshape(n, dt):
        return jax.ShapeDtypeStruct((m, n), dt)

    return pl.pallas_call(
        _proj_dsa_kernel,
        grid=(m // tm,),
        in_specs=[x_spec, mod_spec, mod_spec, _full(wq.shape), _full(wk.shape), _full(wv.shape),
                  _full(wi.shape), _full(wm.shape), _full(gq.shape), _full(gk.shape), _full(bf.shape),
                  tab_spec, tab_spec, tab_spec, tab_spec],
        out_specs=[ospec(nq), ospec(nk), ospec(nk), ospec(ni), ospec(LANES), ospec(LANES),
                   ospec(nk), ospec(nk), ospec(LANES)],
        out_shape=[oshape(nq, BF16), oshape(nk, BF16), oshape(nk, BF16), oshape(ni, BF16),
                   oshape(LANES, BF16), oshape(LANES, BF16),
                   oshape(nk, F32), oshape(nk, F32), oshape(LANES, F32)],
        compiler_params=_cp(("parallel",)),
        name="proj_dsa",
    )(x2d, sc, sh, wq, wk, wv, wi, wm, gq, gk, bf, *tabs)


def _cumsum_kernel(x_ref, o_ref):
    x = x_ref[...]
    n = x.shape[1]
    lane = lax.broadcasted_iota(I32, x.shape, 1)
    k = 1
    while k < n:
        x = x + jnp.where(lane >= k, pltpu.roll(x, k, axis=1), 0.0)
        k *= 2
    o_ref[...] = x


def _cumsum_lanes(x):
    return pl.pallas_call(
        _cumsum_kernel,
        out_shape=jax.ShapeDtypeStruct(x.shape, F32),
        name="logf_cumsum",
    )(x)


def _fox_flash_kernel(q_ref, k_ref, v_ref, cq_ref, ck_ref, o_ref, m_ref, l_ref, acc_ref, *, tq, tk):
    qi = pl.program_id(2)
    ki = pl.program_id(3)

    @pl.when(ki == 0)
    def _():
        m_ref[...] = jnp.full(m_ref.shape, -jnp.inf, F32)
        l_ref[...] = jnp.zeros(l_ref.shape, F32)
        acc_ref[...] = jnp.zeros(acc_ref.shape, F32)

    @pl.when(ki * tk < (qi + 1) * tq)
    def _():
        s = _dot_nt(q_ref[...], k_ref[...]) * (HEAD_DIM ** -0.5)
        s = s + (cq_ref[...] - ck_ref[...])
        row = qi * tq + lax.broadcasted_iota(I32, s.shape, 0)
        col = ki * tk + lax.broadcasted_iota(I32, s.shape, 1)
        s = jnp.where(col <= row, s, -jnp.inf)
        m_prev = m_ref[...]
        m_new = jnp.maximum(m_prev, jnp.max(s, axis=-1, keepdims=True))
        alpha = jnp.exp(m_prev - m_new)
        p = jnp.exp(s - m_new)
        l_ref[...] = alpha * l_ref[...] + jnp.sum(p, axis=-1, keepdims=True)
        acc_ref[...] = alpha * acc_ref[...] + _dot(p.astype(BF16), v_ref[...])
        m_ref[...] = m_new

    @pl.when(ki == pl.num_programs(3) - 1)
    def _():
        o_ref[...] = (acc_ref[...] / l_ref[...]).astype(o_ref.dtype)


def _fox_prompt(qb, kb, vb, cum_col, cum_row, tq=512, tk=512):
    b, s, hd = qb.shape
    h = hd // HEAD_DIM

    def kv_idx(bi, hi, qi, ki):
        return (bi, jnp.minimum(ki, ((qi + 1) * tq - 1) // tk), hi)

    return pl.pallas_call(
        functools.partial(_fox_flash_kernel, tq=tq, tk=tk),
        grid=(b, h, s // tq, s // tk),
        in_specs=[pl.BlockSpec((None, tq, HEAD_DIM), lambda bi, hi, qi, ki: (bi, qi, hi)),
                  pl.BlockSpec((None, tk, HEAD_DIM), kv_idx),
                  pl.BlockSpec((None, tk, HEAD_DIM), kv_idx),
                  pl.BlockSpec((None, None, tq, 1), lambda bi, hi, qi, ki: (bi, hi, qi, 0)),
                  pl.BlockSpec((None, None, 1, tk),
                               lambda bi, hi, qi, ki: (bi, hi, 0, jnp.minimum(ki, ((qi + 1) * tq - 1) // tk)))],
        out_specs=pl.BlockSpec((None, tq, HEAD_DIM), lambda bi, hi, qi, ki: (bi, qi, hi)),
        out_shape=jax.ShapeDtypeStruct((b, s, hd), BF16),
        scratch_shapes=[pltpu.VMEM((tq, 1), F32), pltpu.VMEM((tq, 1), F32), pltpu.VMEM((tq, HEAD_DIM), F32)],
        compiler_params=_cp(("parallel", "parallel", "parallel", "arbitrary")),
        name="fox_prompt",
    )(qb, kb, vb, cum_col, cum_row)


def _order_key(x):
    bits = pltpu.bitcast(x + 0.0, I32)
    return jnp.where(bits < 0, bits ^ jnp.int32(0x7FFFFFFF), bits)


def _kth_largest_key(key, k, reduce_axes):
    shape = list(key.shape)
    for ax in reduce_axes:
        shape[ax] = 1
    sign = jnp.int32(-2 ** 31)

    def body(i, t):
        bit = lax.shift_left(jnp.int32(1), jnp.int32(31) - i)
        cand = t | bit
        ge = jnp.where(key >= (cand ^ sign), 1.0, 0.0)
        cnt = jnp.sum(ge, axis=reduce_axes, keepdims=True)
        return jnp.where(cnt >= float(k), cand, t)

    t = lax.fori_loop(0, 32, body, jnp.zeros(shape, I32))
    return t ^ sign


def _dsa_prompt_kernel(qi_ref, kia_ref, kib_ref, misc_ref, qd_ref, kd_ref, vd_ref, o_ref,
                       score_ref, selb_ref, *, tq, n_sel):
    s_len = kia_ref.shape[0]
    row = pl.program_id(1) * tq + lax.broadcasted_iota(I32, (tq, s_len), 0)
    col = lax.broadcasted_iota(I32, (tq, s_len), 1)
    causal = col <= row

    misc = misc_ref[...]
    kia, kib = kia_ref[...], kib_ref[...]
    acc = jnp.zeros((tq, s_len), F32)
    for p in range(qi_ref.shape[1] // LANES):
        qblk = qi_ref[:, p * LANES:(p + 1) * LANES]
        for half, kmat in enumerate((kia, kib)):
            hh = 2 * p + half
            w = misc[:, MISC_W + hh:MISC_W + hh + 1] * (D_IDX ** -0.5)
            acc = acc + w * jnp.maximum(_dot_nt(qblk, kmat), 0.0)
    score_ref[...] = jnp.where(causal, acc, -jnp.inf)

    key = _order_key(score_ref[...])
    thr = _kth_largest_key(key, n_sel, (1,))
    cnt_gt = jnp.sum(jnp.where(key > thr, 1.0, 0.0), axis=1, keepdims=True)
    cnt_eq = jnp.sum(jnp.where(key == thr, 1.0, 0.0), axis=1, keepdims=True)
    need = float(n_sel) - cnt_gt
    selb_ref[...] = jnp.where((key >= thr) & causal, 0.0, -jnp.inf)
    tie_rows = jnp.where((cnt_eq > need) & (thr > KEY_NEG_INF), 1.0, 0.0)

    @pl.when(jnp.max(tie_rows) > 0.0)
    def _():
        ch = 256
        r_i = lax.broadcasted_iota(I32, (ch, ch), 0)
        c_i = lax.broadcasted_iota(I32, (ch, ch), 1)
        tri = jnp.where(r_i < c_i, 1.0, 0.0).astype(BF16)
        base = jnp.zeros((tq, 1), F32)
        for c in range(s_len // ch):
            sl = slice(c * ch, (c + 1) * ch)
            eq_c = jnp.where(key[:, sl] == thr, 1.0, 0.0)
            rank = base + _dot(eq_c.astype(BF16), tri)
            take = (key[:, sl] > thr) | ((eq_c > 0.0) & (rank < need))
            selb_ref[:, sl] = jnp.where(take & causal[:, sl], 0.0, -jnp.inf)
            base = base + jnp.sum(eq_c, axis=1, keepdims=True)

    n_heads = qd_ref.shape[1] // HEAD_DIM
    rep = n_heads // KV_DSA
    for hh in range(n_heads):
        g = hh // rep
        q = qd_ref[:, hh * HEAD_DIM:(hh + 1) * HEAD_DIM]
        kg = kd_ref[:, g * HEAD_DIM:(g + 1) * HEAD_DIM]
        vg = vd_ref[:, g * HEAD_DIM:(g + 1) * HEAD_DIM]
        s = _dot_nt(q, kg) * (HEAD_DIM ** -0.5) + selb_ref[...]
        m = jnp.max(s, axis=-1, keepdims=True)
        p = jnp.exp(s - m)
        l = jnp.sum(p, axis=-1, keepdims=True)
        o_ref[:, hh * HEAD_DIM:(hh + 1) * HEAD_DIM] = (_dot(p.astype(BF16), vg) / l).astype(o_ref.dtype)


def _dsa_prompt(qi, kia, kib, misc, qd, kd, vd, tq=256):
    b, s, _ = qi.shape
    n_sel = min(TOPK_MAX, s // 4)

    def qspec(n):
        return pl.BlockSpec((None, tq, n), lambda bi, i: (bi, i, 0))

    def kspec(n):
        return pl.BlockSpec((None, s, n), lambda bi, i: (bi, 0, 0))

    return pl.pallas_call(
        functools.partial(_dsa_prompt_kernel, tq=tq, n_sel=n_sel),
        grid=(b, s // tq),
        in_specs=[qspec(qi.shape[2]), kspec(LANES), kspec(LANES), qspec(LANES),
                  qspec(qd.shape[2]), kspec(kd.shape[2]), kspec(vd.shape[2])],
        out_specs=qspec(qd.shape[2]),
        out_shape=jax.ShapeDtypeStruct(qd.shape, BF16),
        scratch_shapes=[pltpu.VMEM((tq, s), F32), pltpu.VMEM((tq, s), F32)],
        compiler_params=_cp(("parallel", "parallel")),
        name="dsa_prompt",
    )(qi, kia, kib, misc, qd, kd, vd)


def _suffix_sum_incl(x):
    n = x.shape[1]
    lane = lax.broadcasted_iota(I32, x.shape, 1)
    k = 1
    while k < n:
        x = x + jnp.where(lane + k < n, pltpu.roll(x, n - k, axis=1), 0.0)
        k *= 2
    return x


def _fox_sample_kernel(pt_ref, q_ref, kn_ref, vn_ref, lfn_ref, *refs, pp, n_heads):
    k_refs = refs[:pp]
    v_refs = refs[pp:2 * pp]
    lf_refs = refs[2 * pp:3 * pp]
    o_ref, m_ref, l_ref, acc_ref, carry_ref = refs[3 * pp:]
    c = pl.program_id(1)
    scale = HEAD_DIM ** -0.5
    q = q_ref[...] * scale
    qb = q.astype(BF16)
    rowid = lax.broadcasted_iota(I32, (n_heads, PAGE_SIZE), 0)

    @pl.when(c == 0)
    def _():
        s_new = jnp.sum(q * kn_ref[...], axis=-1, keepdims=True)
        m_ref[...] = jnp.broadcast_to(s_new, m_ref.shape)
        l_ref[...] = jnp.ones(l_ref.shape, F32)
        acc_ref[...] = vn_ref[...]
        carry_ref[...] = lfn_ref[...]

    carry = carry_ref[...]
    logits = [None] * pp
    for j in reversed(range(pp)):
        lf = lf_refs[j][...]
        suf = _suffix_sum_incl(lf)
        bias = carry + (suf - lf)
        carry = carry + jnp.sum(lf, axis=-1, keepdims=True)
        s = jnp.zeros((n_heads, PAGE_SIZE), F32)
        for hh in range(n_heads):
            kh = k_refs[j][pl.ds(hh, PAGE_SIZE, stride=n_heads), :].astype(BF16)
            s = jnp.where(rowid == hh, _dot_nt(qb, kh), s)
        logits[j] = s + bias
    carry_ref[...] = carry

    m_prev = m_ref[...]
    m_new = m_prev
    for j in range(pp):
        m_new = jnp.maximum(m_new, jnp.max(logits[j], axis=-1, keepdims=True))
    alpha = jnp.exp(m_prev - m_new)
    l_new = alpha * l_ref[...]
    acc = alpha * acc_ref[...]
    for j in range(pp):
        p = jnp.exp(logits[j] - m_new)
        l_new = l_new + jnp.sum(p, axis=-1, keepdims=True)
        pb = p.astype(BF16)
        for hh in range(n_heads):
            vh = v_refs[j][pl.ds(hh, PAGE_SIZE, stride=n_heads), :].astype(BF16)
            acc = acc + jnp.where(rowid == hh, _dot(pb, vh), 0.0)
    m_ref[...] = m_new
    l_ref[...] = l_new
    acc_ref[...] = acc

    @pl.when(c == pl.num_programs(1) - 1)
    def _():
        o_ref[...] = acc / l_new


def _fox_sample(page_table, q, k_new, v_new, lf_new, cache_k, cache_v, cache_lf_t, pp=4):
    db, n_heads, _ = q.shape
    n_pages = page_table.shape[1]
    nc = n_pages // pp

    def page_idx(j):
        return lambda bi, ci, pt: (pt[bi, (nc - 1 - ci) * pp + j], 0, 0)

    vec_spec = pl.BlockSpec((None, n_heads, HEAD_DIM), lambda bi, ci, pt: (bi, 0, 0))
    kv_specs = [pl.BlockSpec((None, PAGE_SIZE * n_heads, HEAD_DIM), page_idx(j)) for j in range(pp)]
    lf_specs = [pl.BlockSpec((None, n_heads, PAGE_SIZE), page_idx(j)) for j in range(pp)]
    grid_spec = pltpu.PrefetchScalarGridSpec(
        num_scalar_prefetch=1,
        grid=(db, nc),
        in_specs=[vec_spec] * 4 + kv_specs + kv_specs + lf_specs,
        out_specs=vec_spec,
        scratch_shapes=[pltpu.VMEM((n_heads, HEAD_DIM), F32)] * 4,
    )
    return pl.pallas_call(
        functools.partial(_fox_sample_kernel, pp=pp, n_heads=n_heads),
        grid_spec=grid_spec,
        out_shape=jax.ShapeDtypeStruct((db, n_heads, HEAD_DIM), F32),
        compiler_params=_cp(("parallel", "arbitrary")),
        name="fox_sample",
    )(page_table, q, k_new, v_new, lf_new, *([cache_k] * pp), *([cache_v] * pp), *([cache_lf_t] * pp))


def _idx_score_kernel(pt_ref, qi_ref, w_ref, kin_ref, *refs, pp):
    k_refs = refs[:pp]
    o_ref = refs[pp]
    c = pl.program_id(1)
    last = pl.num_programs(1) - 1
    qi = qi_ref[...]
    w = w_ref[...] * (D_IDX ** -0.5)

    @pl.when(c < last)
    def _():
        qb = qi.astype(BF16)
        rows = []
        for j in range(pp):
            d = _dot(qb, k_refs[j][...].astype(BF16))
            rows.append(jnp.sum(w * jnp.maximum(d, 0.0), axis=0, keepdims=True))
        o_ref[...] = jnp.concatenate(rows, axis=0)

    @pl.when(c == last)
    def _():
        d = jnp.sum(qi * kin_ref[...], axis=-1, keepdims=True)
        sc = jnp.sum(w * jnp.maximum(d, 0.0), axis=0, keepdims=True)
        r = lax.broadcasted_iota(I32, o_ref.shape, 0)
        l = lax.broadcasted_iota(I32, o_ref.shape, 1)
        o_ref[...] = jnp.where((r == 0) & (l == 0), jnp.broadcast_to(sc, o_ref.shape), -jnp.inf)


def _idx_scores_sample(page_table, qi, w, ki_new, cache_idx, pp=8):
    db = qi.shape[0]
    n_pages = page_table.shape[1]
    nc = n_pages // pp

    def page_idx(j):
        return lambda bi, ci, pt: (pt[bi, jnp.minimum(ci, nc - 1) * pp + j], 0, 0)

    grid_spec = pltpu.PrefetchScalarGridSpec(
        num_scalar_prefetch=1,
        grid=(db, nc + 1),
        in_specs=[pl.BlockSpec((None, H_IDX, D_IDX), lambda bi, ci, pt: (bi, 0, 0)),
                  pl.BlockSpec((None, H_IDX, LANES), lambda bi, ci, pt: (bi, 0, 0)),
                  pl.BlockSpec((None, 1, D_IDX), lambda bi, ci, pt: (bi, 0, 0))]
                 + [pl.BlockSpec((None, D_IDX, PAGE_SIZE), page_idx(j)) for j in range(pp)],
        out_specs=pl.BlockSpec((None, pp, PAGE_SIZE), lambda bi, ci, pt: (bi, ci, 0)),
    )
    return pl.pallas_call(
        functools.partial(_idx_score_kernel, pp=pp),
        grid_spec=grid_spec,
        out_shape=jax.ShapeDtypeStruct((db, n_pages + pp, PAGE_SIZE), F32),
        compiler_params=_cp(("parallel", "arbitrary")),
        name="idx_scores_sample",
    )(page_table, qi, w, ki_new, *([cache_idx] * pp))


def _select_kernel(s_ref, o_ref, *, n_sel):
    score = s_ref[...]
    rows = score.shape[0]
    key = _order_key(score)
    thr = _kth_largest_key(key, n_sel, (0, 1))
    gt = key > thr
    eq = jnp.where(key == thr, 1.0, 0.0)
    need = float(n_sel) - jnp.sum(jnp.where(gt, 1.0, 0.0), axis=(0, 1), keepdims=True)
    r_i = lax.broadcasted_iota(I32, (LANES, LANES), 0)
    c_i = lax.broadcasted_iota(I32, (LANES, LANES), 1)
    within = _dot(eq.astype(BF16), jnp.where(r_i < c_i, 1.0, 0.0).astype(BF16))
    row_cnt = jnp.broadcast_to(jnp.sum(eq, axis=1, keepdims=True), score.shape)
    rows_pad = 2 * LANES
    cnt_pad = jnp.concatenate([row_cnt, jnp.zeros((rows_pad - rows, LANES), F32)], axis=0)
    rr = lax.broadcasted_iota(I32, (rows_pad, rows_pad), 0)
    rc = lax.broadcasted_iota(I32, (rows_pad, rows_pad), 1)
    before = _dot(jnp.where(rc < rr, 1.0, 0.0).astype(BF16), cnt_pad.astype(BF16))[:rows]
    take = gt | ((eq > 0.0) & (within + before < need))
    o_ref[...] = jnp.where(take & (score > -jnp.inf), 0.0, NEG)


def _select_sample(scores, n_sel):
    db, rows, lanes = scores.shape
    spec = pl.BlockSpec((None, rows, lanes), lambda bi: (bi, 0, 0))
    return pl.pallas_call(
        functools.partial(_select_kernel, n_sel=n_sel),
        grid=(db,),
        in_specs=[spec],
        out_specs=spec,
        out_shape=jax.ShapeDtypeStruct(scores.shape, F32),
        compiler_params=_cp(("parallel",)),
        name="select_sample",
    )(scores)


def _dsa_sample_kernel(pt_ref, q_ref, kn_ref, vn_ref, selb_ref, selbn_ref, *refs, pp, n_heads):
    k_refs = refs[:pp]
    v_refs = refs[pp:2 * pp]
    o_ref, m_ref, l_ref, acc_ref = refs[2 * pp:]
    c = pl.program_id(1)
    rep = n_heads // KV_DSA
    q = q_ref[...] * (HEAD_DIM ** -0.5)
    qb = q.astype(BF16)
    rowid = lax.broadcasted_iota(I32, (n_heads, PAGE_SIZE), 0)
    grp = rowid // rep

    @pl.when(c == 0)
    def _():
        kn = kn_ref[...]
        vn = vn_ref[...]
        s_new = jnp.zeros((n_heads, 1), F32)
        v_rows = jnp.zeros((n_heads, HEAD_DIM), F32)
        for g in range(KV_DSA):
            sg = jnp.sum(q * kn[g:g + 1, :], axis=-1, keepdims=True)
            s_new = jnp.where(grp[:, :1] == g, sg, s_new)
            v_rows = jnp.where(grp == g, jnp.broadcast_to(vn[g:g + 1, :], v_rows.shape), v_rows)
        s_new = s_new + selbn_ref[0:1, 0:1]
        m_ref[...] = jnp.broadcast_to(jnp.maximum(s_new, NEG), m_ref.shape)
        p_new = jnp.exp(s_new - m_ref[:, :1])
        l_ref[...] = jnp.broadcast_to(p_new, l_ref.shape)
        acc_ref[...] = p_new * v_rows

    selb = selb_ref[...]
    logits = []
    for j in range(pp):
        s = jnp.zeros((n_heads, PAGE_SIZE), F32)
        for g in range(KV_DSA):
            kg = k_refs[j][pl.ds(g, PAGE_SIZE, stride=KV_DSA), :].astype(BF16)
            s = jnp.where(grp == g, _dot_nt(qb, kg), s)
        logits.append(s + selb[j:j + 1, :])

    m_prev = m_ref[...]
    m_new = m_prev
    for j in range(pp):
        m_new = jnp.maximum(m_new, jnp.max(logits[j], axis=-1, keepdims=True))
    alpha = jnp.exp(m_prev - m_new)
    l_new = alpha * l_ref[...]
    acc = alpha * acc_ref[...]
    for j in range(pp):
        p = jnp.exp(logits[j] - m_new)
        l_new = l_new + jnp.sum(p, axis=-1, keepdims=True)
        pb = p.astype(BF16)
        for g in range(KV_DSA):
            vg = v_refs[j][pl.ds(g, PAGE_SIZE, stride=KV_DSA), :].astype(BF16)
            acc = acc + jnp.where(grp == g, _dot(pb, vg), 0.0)
    m_ref[...] = m_new
    l_ref[...] = l_new
    acc_ref[...] = acc

    @pl.when(c == pl.num_programs(1) - 1)
    def _():
        o_ref[...] = acc / l_new


def _dsa_sample(page_table, q, k_new, v_new, selb, cache_k, cache_v, pp=8):
    db, n_heads, _ = q.shape
    n_pages = page_table.shape[1]
    nc = n_pages // pp

    def page_idx(j):
        return lambda bi, ci, pt: (pt[bi, ci * pp + j], 0, 0)

    q_spec = pl.BlockSpec((None, n_heads, HEAD_DIM), lambda bi, ci, pt: (bi, 0, 0))
    n_spec = pl.BlockSpec((None, KV_DSA, HEAD_DIM), lambda bi, ci, pt: (bi, 0, 0))
    kv_specs = [pl.BlockSpec((None, PAGE_SIZE * KV_DSA, HEAD_DIM), page_idx(j)) for j in range(pp)]
    grid_spec = pltpu.PrefetchScalarGridSpec(
        num_scalar_prefetch=1,
        grid=(db, nc),
        in_specs=[q_spec, n_spec, n_spec,
                  pl.BlockSpec((None, pp, PAGE_SIZE), lambda bi, ci, pt: (bi, ci, 0)),
                  pl.BlockSpec((None, pp, PAGE_SIZE), lambda bi, ci, pt: (bi, nc, 0))]
                 + kv_specs + kv_specs,
        out_specs=q_spec,
        scratch_shapes=[pltpu.VMEM((n_heads, HEAD_DIM), F32)] * 3,
    )
    return pl.pallas_call(
        functools.partial(_dsa_sample_kernel, pp=pp, n_heads=n_heads),
        grid_spec=grid_spec,
        out_shape=jax.ShapeDtypeStruct((db, n_heads, HEAD_DIM), F32),
        compiler_params=_cp(("parallel", "arbitrary")),
        name="dsa_sample",
    )(page_table, q, k_new, v_new, selb, selb, *([cache_k] * pp), *([cache_v] * pp))


def _outproj_kernel(*refs, aliased, n_real):
    x1_ref, h2_ref, route_ref = refs[9 + aliased:]

    @pl.when(pl.program_id(0) < n_real)
    def _():
        _outproj_body(*refs[:9], x1_ref, h2_ref, route_ref)

    @pl.when(pl.program_id(0) >= n_real)
    def _():
        h2_ref[...] = jnp.zeros(h2_ref.shape, F32)
        route_ref[...] = jnp.zeros(route_ref.shape, F32)


def _outproj_body(x_ref, of_ref, od_ref, wf_ref, wd_ref, ga_ref, sc_ref, sh_ref, wr_ref,
                  x1_ref, h2_ref, route_ref):
    mix = _dot(of_ref[...].astype(BF16), wf_ref[...]) + _dot(od_ref[...].astype(BF16), wd_ref[...])
    x1 = x_ref[...] + ga_ref[...] * mix
    x1_ref[...] = x1
    h2 = _modulated(x1, sc_ref, sh_ref)
    h2_ref[...] = h2
    logits = _dot(h2.astype(BF16), wr_ref[...])
    lane = lax.broadcasted_iota(I32, logits.shape, 1).astype(F32)

    def first_argmax(vals, mask):
        v = jnp.where(mask, vals, -jnp.inf)
        mx = jnp.max(v, axis=-1, keepdims=True)
        idx = jnp.min(jnp.where(v == mx, lane, float(LANES)), axis=-1, keepdims=True)
        return mx, idx

    is_g = lane < float(N_GROUPS)
    mg, g_sel = first_argmax(logits, is_g)
    p_gsel = 1.0 / jnp.sum(jnp.where(is_g, jnp.exp(logits - mg), 0.0), axis=-1, keepdims=True)
    lo = float(N_GROUPS) + g_sel * float(EXPERTS_PER_GROUP)
    in_grp = (lane >= lo) & (lane < lo + float(EXPERTS_PER_GROUP))
    m1, i1 = first_argmax(logits, in_grp)
    m2, i2 = first_argmax(logits, in_grp & (lane != i1))
    e2 = jnp.exp(m2 - m1)
    w1 = p_gsel / (1.0 + e2)
    w2 = p_gsel * e2 / (1.0 + e2)
    route_ref[...] = jnp.where(lane == 0.0, i1 - float(N_GROUPS),
                               jnp.where(lane == 1.0, i2 - float(N_GROUPS),
                                         jnp.where(lane == 2.0, w1, jnp.where(lane == 3.0, w2, 0.0))))


def _outproj(x2d, o_fox, o_dsa, wf, wd, ga, sc, sh, wr, tm, rows_per_group, total_rows, row_offset,
             prev=None):
    m, d = x2d.shape
    n_real = m // tm
    tiles_per_group = rows_per_group // tm
    off = row_offset // tm
    half = o_fox.shape[1]
    n_tail = 0 if prev is not None else (total_rows - row_offset - m) // tm

    def row_idx(i):
        return (jnp.minimum(i, n_real - 1), 0)

    def mod_idx(i):
        return (jnp.minimum(i, n_real - 1) // tiles_per_group, 0, 0)

    mod_spec = pl.BlockSpec((None, sc.shape[1], d), mod_idx)
    in_specs = [pl.BlockSpec((tm, d), row_idx), pl.BlockSpec((tm, half), row_idx), pl.BlockSpec((tm, half), row_idx),
                _full(wf.shape), _full(wd.shape), mod_spec, mod_spec, mod_spec, _full(wr.shape)]
    args = [x2d, o_fox, o_dsa, wf, wd, ga, sc, sh, wr]
    aliases = {}
    if prev is not None:
        in_specs += [pl.BlockSpec(memory_space=pl.ANY), pl.BlockSpec(memory_space=pl.ANY)]
        args += list(prev)
        aliases = {9: 1, 10: 2}
    return pl.pallas_call(
        functools.partial(_outproj_kernel, aliased=0 if prev is None else 2, n_real=n_real),
        grid=(n_real + n_tail,),
        in_specs=in_specs,
        out_specs=[pl.BlockSpec((tm, d), row_idx),
                   pl.BlockSpec((tm, d), lambda i: (i + off, 0)),
                   pl.BlockSpec((tm, LANES), lambda i: (i + off, 0))],
        out_shape=[jax.ShapeDtypeStruct((m, d), F32),
                   jax.ShapeDtypeStruct((total_rows, d), F32),
                   jax.ShapeDtypeStruct((total_rows, LANES), F32)],
        input_output_aliases=aliases,
        compiler_params=_cp(("arbitrary",)),
        name="outproj_router",
    )(*args)


def _moe_kernel(te_ref, ta_ref, ids_ref, h_hbm, wg_ref, wu_ref, wd_ref, o_ref, hbuf, sem, *, tm):
    t = pl.program_id(0)

    def row_copy(r):
        return pltpu.make_async_copy(h_hbm.at[pl.ds(ids_ref[0, r], 1)], hbuf.at[pl.ds(r, 1)], sem)

    @pl.when(ta_ref[t] > 0)
    def _():
        def issue(r, carry):
            row_copy(r).start()
            return carry

        def wait(r, carry):
            row_copy(r).wait()
            return carry

        lax.fori_loop(0, tm, issue, 0)
        lax.fori_loop(0, tm, wait, 0)
        h = hbuf[...].astype(BF16)
        a = _dot(h, wg_ref[...].astype(BF16))
        u = _dot(h, wu_ref[...].astype(BF16))
        act = (a / (1.0 + jnp.exp(-a))) * u
        o_ref[...] = _dot(act.astype(BF16), wd_ref[...].astype(BF16))

    @pl.when(ta_ref[t] == 0)
    def _():
        o_ref[...] = jnp.zeros(o_ref.shape, F32)


def _moe_experts(tile_expert, tile_active, slot_token, h_all, w_gate, w_up, w_down, tm):
    n_tiles = tile_expert.shape[0]
    d = h_all.shape[1]
    f = w_gate.shape[2]
    grid_spec = pltpu.PrefetchScalarGridSpec(
        num_scalar_prefetch=2,
        grid=(n_tiles,),
        in_specs=[pl.BlockSpec((None, 1, tm), lambda t, te, ta: (t, 0, 0), memory_space=pltpu.SMEM),
                  pl.BlockSpec(memory_space=pl.ANY),
                  pl.BlockSpec((None, d, f), lambda t, te, ta: (te[t], 0, 0)),
                  pl.BlockSpec((None, d, f), lambda t, te, ta: (te[t], 0, 0)),
                  pl.BlockSpec((None, f, d), lambda t, te, ta: (te[t], 0, 0))],
        out_specs=pl.BlockSpec((tm, d), lambda t, te, ta: (t, 0)),
        scratch_shapes=[pltpu.VMEM((tm, d), F32), pltpu.SemaphoreType.DMA(())],
    )
    return pl.pallas_call(
        functools.partial(_moe_kernel, tm=tm),
        grid_spec=grid_spec,
        out_shape=jax.ShapeDtypeStruct((n_tiles * tm, d), F32),
        compiler_params=_cp(("arbitrary",)),
        name="moe_experts",
    )(tile_expert, tile_active, slot_token.reshape(n_tiles, 1, tm), h_all, w_gate, w_up, w_down)


def _combine_kernel(s0_ref, s1_ref, x1_ref, gm_ref, route_ref, y_hbm, o_ref, b0, b1, sem, *, tm):
    def copies(r):
        return (pltpu.make_async_copy(y_hbm.at[pl.ds(s0_ref[0, r], 1)], b0.at[pl.ds(r, 1)], sem),
                pltpu.make_async_copy(y_hbm.at[pl.ds(s1_ref[0, r], 1)], b1.at[pl.ds(r, 1)], sem))

    def issue(r, carry):
        c0, c1 = copies(r)
        c0.start()
        c1.start()
        return carry

    def wait(r, carry):
        c0, c1 = copies(r)
        c0.wait()
        c1.wait()
        return carry

    lax.fori_loop(0, tm, issue, 0)
    lax.fori_loop(0, tm, wait, 0)
    route = route_ref[...]
    moe = route[:, 2:3] * b0[...] + route[:, 3:4] * b1[...]
    o_ref[...] = x1_ref[...] + gm_ref[...] * moe


def _combine(slot0, slot1, x1, gm, route_all, y_slots, tm, rows_per_group, row_offset):
    m, d = x1.shape
    x_spec, mod_spec = _row_specs(tm, d, rows_per_group, gm.shape[1])
    off = row_offset // tm
    n_tiles = m // tm
    id_spec = pl.BlockSpec((None, 1, tm), lambda i: (i, 0, 0), memory_space=pltpu.SMEM)
    return pl.pallas_call(
        functools.partial(_combine_kernel, tm=tm),
        grid=(n_tiles,),
        in_specs=[id_spec, id_spec, x_spec, mod_spec,
                  pl.BlockSpec((tm, LANES), lambda i: (i + off, 0)),
                  pl.BlockSpec(memory_space=pl.ANY)],
        out_specs=x_spec,
        out_shape=jax.ShapeDtypeStruct((m, d), F32),
        scratch_shapes=[pltpu.VMEM((tm, d), F32), pltpu.VMEM((tm, d), F32), pltpu.SemaphoreType.DMA(())],
        compiler_params=_cp(("arbitrary",)),
        name="moe_combine",
    )(slot0.reshape(n_tiles, 1, tm), slot1.reshape(n_tiles, 1, tm), x1, gm, route_all, y_slots)


def _moe_plan(route_all, tm):
    t_all = route_all.shape[0]
    eid = route_all[:, :2].astype(I32)
    flat = eid.reshape(-1)
    onehot = (flat[:, None] == jnp.arange(N_EXPERTS, dtype=I32)[None, :]).astype(I32)
    rank = jnp.sum((jnp.cumsum(onehot, axis=0) - onehot) * onehot, axis=1)
    counts = jnp.sum(onehot, axis=0)
    padded = ((counts + tm - 1) // tm) * tm
    ends = jnp.cumsum(padded)
    starts = ends - padded
    slot = starts[flat] + rank
    n_tiles = (2 * t_all + N_EXPERTS * (tm - 1)) // tm + 1
    tile_start = jnp.arange(n_tiles, dtype=I32) * tm
    tile_expert = jnp.minimum(jnp.sum((tile_start[:, None] >= ends[None, :]).astype(I32), axis=1),
                              N_EXPERTS - 1)
    tile_active = (tile_start < ends[-1]).astype(I32)
    token_of_pair = jnp.arange(2 * t_all, dtype=I32) // 2
    slot_token = jnp.zeros((n_tiles * tm,), I32).at[slot].set(token_of_pair)
    slots = slot.reshape(t_all, 2)
    return tile_expert.astype(I32), tile_active, slot_token, slots[:, 0], slots[:, 1]


def _rope_tables(pos, dim, lanes):
    half = dim // 2
    inv = ROPE_THETA ** (-jnp.arange(half, dtype=F32) / half)
    ang = pos.astype(F32)[:, None] * inv[None, :]
    cos, sin = jnp.cos(ang), jnp.sin(ang)
    reps = lanes // dim
    cos_t = jnp.tile(jnp.concatenate([cos, cos], axis=-1), (1, reps))
    sin_t = jnp.tile(jnp.concatenate([-sin, sin], axis=-1), (1, reps))
    return cos_t, sin_t


def kernel(x_prompt, x_sample, cache_fox_k, cache_fox_v, cache_fox_logf, cache_dsa_k, cache_dsa_v,
           cache_idx_k, page_table, c_prompt, c_sample, w_in, b_forget, q_gain_fox, k_gain_fox,
           q_gain_dsa, k_gain_dsa, w_out, w_ada, b_ada, w_router_group, w_router_expert,
           w_gate, w_up, w_down):
    b, s, d = x_prompt.shape
    db, dq, _ = x_sample.shape
    depth = w_in.shape[0]
    assert depth == 1 and dq == 1
    past_len = page_table.shape[1] * PAGE_SIZE
    n_phys = cache_fox_k.shape[1]
    h_fox = cache_fox_k.shape[3]
    h_dsa = d // (2 * HEAD_DIM)
    t_p, t_s = b * s, db * dq
    t_all = t_p + t_s
    tm_p, tm_s = 256, t_s
    layer = 0

    n_c = b + db
    pad = (-n_c) % 8
    c_all = jnp.concatenate([c_prompt, c_sample, jnp.zeros((pad, d), F32)], axis=0)
    mod = _adaln(c_all, w_ada[layer], b_ada[layer])
    mods = [mod[:, i * d:(i + 1) * d] for i in range(6)]
    mod_p = [m_[:b].reshape(b, 1, d) for m_ in mods]
    mod_s = [m_[b:b + db].reshape(1, db, d) for m_ in mods]

    w = w_in[layer]
    nf = h_fox * HEAD_DIM
    nd = h_dsa * HEAD_DIM
    nkv = KV_DSA * HEAD_DIM
    ni = H_IDX * D_IDX
    o = 0
    wqf = w[:, o:o + nf].astype(BF16); o += nf
    wkf = w[:, o:o + nf].astype(BF16); o += nf
    wvf = w[:, o:o + nf].astype(BF16); o += nf
    w_fl = w[:, o:o + h_fox]; o += h_fox
    wqd = w[:, o:o + nd].astype(BF16); o += nd
    wkd = w[:, o:o + nkv].astype(BF16); o += nkv
    wvd = w[:, o:o + nkv].astype(BF16); o += nkv
    wqi = w[:, o:o + ni].astype(BF16); o += ni
    w_ki = w[:, o:o + D_IDX]; o += D_IDX
    w_wi = w[:, o:o + H_IDX]; o += H_IDX
    assert h_fox <= MISC_W - MISC_F
    wmisc = jnp.concatenate([w_ki, w_fl, jnp.zeros((d, MISC_W - MISC_F - h_fox), F32), w_wi,
                             jnp.zeros((d, LANES - MISC_END), F32)], axis=1).astype(BF16)
    bf = jnp.zeros((1, LANES), F32).at[0, MISC_F:MISC_F + h_fox].set(b_forget[layer])
    gqf, gkf = q_gain_fox[layer].reshape(1, HEAD_DIM), k_gain_fox[layer].reshape(1, HEAD_DIM)
    gqd, gkd = q_gain_dsa[layer].reshape(1, HEAD_DIM), k_gain_dsa[layer].reshape(1, HEAD_DIM)

    pos_p = jnp.arange(s, dtype=I32)
    pos_s = jnp.full((t_s,), past_len, I32)
    tabs_p = _rope_tables(pos_p, HEAD_DIM, LANES) + _rope_tables(pos_p, D_IDX, LANES)
    tabs_s = _rope_tables(pos_s, HEAD_DIM, LANES) + _rope_tables(pos_s, D_IDX, LANES)

    xp2 = x_prompt.reshape(t_p, d)
    xs2 = x_sample.reshape(t_s, d)

    def project(x2d, mod_, tabs, tm, rows_per_group, tiles_per_seq):
        fox = _proj_fox(x2d, mod_[1], mod_[0], wqf, wkf, wvf, gqf, gkf, tm, rows_per_group)
        dsa = _proj_dsa(x2d, mod_[1], mod_[0], wqd, wkd, wvd, wqi, wmisc, gqd, gkd, bf, tabs,
                        tm, rows_per_group, tiles_per_seq)
        return fox, dsa

    (qf_p, kfb_p, vfb_p, kf_p, vf_p), (qd_p, kdb_p, vdb_p, qi_p, kia_p, kib_p, kd_p, vd_p, misc_p) = \
        project(xp2, mod_p, tabs_p, tm_p, s, s // tm_p)
    (qf_s, _, _, kf_s, vf_s), (qd_s, _, _, qi_s, _, _, kd_s, vd_s, misc_s) = \
        project(xs2, mod_s, tabs_s, tm_s, t_s, 1)

    lf_p = misc_p[:, MISC_F:MISC_F + h_fox].reshape(b, s, h_fox)
    cum = _cumsum_lanes(lf_p.transpose(0, 2, 1).reshape(b * h_fox, s)).reshape(b, h_fox, s)
    o_fox_p = _fox_prompt(qf_p.reshape(b, s, nf), kfb_p.reshape(b, s, nf), vfb_p.reshape(b, s, nf),
                          cum.reshape(b, h_fox, s, 1), cum.reshape(b, h_fox, 1, s))
    o_dsa_p = _dsa_prompt(qi_p.reshape(b, s, ni), kia_p.reshape(b, s, LANES), kib_p.reshape(b, s, LANES),
                          misc_p.reshape(b, s, LANES), qd_p.reshape(b, s, nd),
                          kdb_p.reshape(b, s, nkv), vdb_p.reshape(b, s, nkv))

    lf_s = misc_s[:, MISC_F:MISC_F + h_fox]
    ck = cache_fox_k[layer].reshape(n_phys, PAGE_SIZE * h_fox, HEAD_DIM)
    cv = cache_fox_v[layer].reshape(n_phys, PAGE_SIZE * h_fox, HEAD_DIM)
    clf = cache_fox_logf[layer].astype(F32).transpose(0, 2, 1)
    o_fox_s = _fox_sample(page_table,
                          qf_s.astype(F32).reshape(db, h_fox, HEAD_DIM),
                          kf_s.reshape(db, h_fox, HEAD_DIM), vf_s.reshape(db, h_fox, HEAD_DIM),
                          jnp.broadcast_to(lf_s[:, :, None], (db, h_fox, HEAD_DIM)),
                          ck, cv, clf)
    w_s = misc_s[:, MISC_W:MISC_W + H_IDX]
    scores = _idx_scores_sample(page_table,
                                qi_s.astype(F32).reshape(db, H_IDX, D_IDX),
                                jnp.broadcast_to(w_s[:, :, None], (db, H_IDX, LANES)),
                                misc_s[:, MISC_KI:MISC_KI + D_IDX].reshape(db, 1, D_IDX),
                                cache_idx_k[layer].transpose(0, 2, 1))
    n_sel = min(TOPK_MAX, (past_len + dq) // 4)
    selb = _select_sample(scores, n_sel)
    cdk = cache_dsa_k[layer].reshape(n_phys, PAGE_SIZE * KV_DSA, HEAD_DIM)
    cdv = cache_dsa_v[layer].reshape(n_phys, PAGE_SIZE * KV_DSA, HEAD_DIM)
    o_dsa_s = _dsa_sample(page_table, qd_s.astype(F32).reshape(db, h_dsa, HEAD_DIM),
                          kd_s.reshape(db, KV_DSA, HEAD_DIM), vd_s.reshape(db, KV_DSA, HEAD_DIM),
                          selb, cdk, cdv)

    wo = w_out[layer].astype(BF16)
    wof, wod = wo[:nf], wo[nf:]
    wr = jnp.concatenate([w_router_group[layer],
                          w_router_expert[layer].transpose(1, 0, 2).reshape(d, N_EXPERTS),
                          jnp.zeros((d, LANES - N_GROUPS - N_EXPERTS), F32)], axis=1).astype(BF16)
    rows_all = t_p + ((t_s + tm_p - 1) // tm_p) * tm_p
    x1_p, h2_all, route_all = _outproj(xp2, o_fox_p.reshape(t_p, nf), o_dsa_p.reshape(t_p, nd), wof, wod,
                                       mod_p[2], mod_p[4], mod_p[3], wr, tm_p, s, rows_all, 0)
    x1_s, h2_all, route_all = _outproj(xs2, o_fox_s.reshape(t_s, nf), o_dsa_s.reshape(t_s, nd), wof, wod,
                                       mod_s[2], mod_s[4], mod_s[3], wr, tm_s, t_s, rows_all, t_p,
                                       prev=(h2_all, route_all))

    tile_expert, tile_active, slot_token, slot0, slot1 = _moe_plan(route_all[:t_all], tm_p)
    y_slots = _moe_experts(tile_expert, tile_active, slot_token, h2_all,
                           w_gate[layer], w_up[layer], w_down[layer], tm_p)
    y_p = _combine(slot0[:t_p], slot1[:t_p], x1_p, mod_p[5], route_all, y_slots, tm_p, s, 0)
    y_s = _combine(slot0[t_p:], slot1[t_p:], x1_s, mod_s[5], route_all, y_slots, tm_s, t_s, t_p)

    def rows(kf, vf, misc, kd, vd, g, t):
        return (kf.reshape(1, g, t, h_fox, HEAD_DIM), vf.reshape(1, g, t, h_fox, HEAD_DIM),
                misc[:, MISC_F:MISC_F + h_fox].reshape(1, g, t, h_fox),
                kd.reshape(1, g, t, KV_DSA, HEAD_DIM), vd.reshape(1, g, t, KV_DSA, HEAD_DIM),
                misc[:, MISC_KI:MISC_KI + D_IDX].reshape(1, g, t, D_IDX))

    return ((y_p.reshape(b, s, d), y_s.reshape(db, dq, d))
            + rows(kf_p, vf_p, misc_p, kd_p, vd_p, b, s)
            + rows(kf_s, vf_s, misc_s, kd_s, vd_s, db, dq))
```

```python
import functools

import jax
import jax.numpy as jnp
from jax import lax
from jax.experimental import pallas as pl
from jax.experimental.pallas import tpu as pltpu

F32 = jnp.float32
BF16 = jnp.bfloat16
I32 = jnp.int32

HEAD_DIM = 128
D_IDX = 64
H_IDX = 16
KV_DSA = 2
TOPK_MAX = 256
N_GROUPS = 4
EXPERTS_PER_GROUP = 8
N_EXPERTS = N_GROUPS * EXPERTS_PER_GROUP
ROPE_THETA = 10000.0
EPS = 1e-6
PAGE_SIZE = 128

LANES = 128
NEG = -1e30
KEY_NEG_INF = -2139095041
VMEM_LIMIT = 52 * 1024 * 1024

MISC_KI = 0
MISC_F = 64
MISC_W = 72
MISC_END = 88


def _cp(sem):
    return pltpu.CompilerParams(dimension_semantics=sem, vmem_limit_bytes=VMEM_LIMIT)


def _dot(a, b):
    return jnp.dot(a, b, preferred_element_type=F32)


def _dot_nt(a, b):
    return lax.dot_general(a, b, (((1,), (1,)), ((), ())), preferred_element_type=F32)


def _adaln_kernel(c_ref, w_ref, b_ref, o_ref):
    c = c_ref[...]
    s = c / (1.0 + jnp.exp(-c))
    o_ref[...] = _dot(s.astype(BF16), w_ref[...].astype(BF16)) + b_ref[...]


def _adaln(c_all, w_ada, b_ada):
    rows, d = c_all.shape
    n = w_ada.shape[1]
    tn = 1024
    return pl.pallas_call(
        _adaln_kernel,
        grid=(n // tn,),
        in_specs=[pl.BlockSpec((rows, d), lambda j: (0, 0)),
                  pl.BlockSpec((d, tn), lambda j: (0, j)),
                  pl.BlockSpec((1, tn), lambda j: (0, j))],
        out_specs=pl.BlockSpec((rows, tn), lambda j: (0, j)),
        out_shape=jax.ShapeDtypeStruct((rows, n), F32),
        compiler_params=_cp(("parallel",)),
        name="adaln",
    )(c_all, w_ada, b_ada.reshape(1, n))


def _modulated(x, sc_ref, sh_ref):
    ms = jnp.mean(x * x, axis=-1, keepdims=True)
    return (x * lax.rsqrt(ms + EPS)) * (1.0 + sc_ref[...]) + sh_ref[...]


def _rms_heads(acc, gain, n_heads):
    outs = []
    for hh in range(n_heads):
        blk = acc[:, hh * HEAD_DIM:(hh + 1) * HEAD_DIM]
        ms = jnp.mean(blk * blk, axis=-1, keepdims=True)
        outs.append(blk * lax.rsqrt(ms + EPS) * gain)
    return outs


def _rope128(y, cos, sin_signed):
    return y * cos + pltpu.roll(y, HEAD_DIM // 2, axis=1) * sin_signed


def _rope64(y, cos, sin_signed, first_half):
    swapped = jnp.where(first_half, pltpu.roll(y, LANES - D_IDX // 2, axis=1),
                        pltpu.roll(y, D_IDX // 2, axis=1))
    return y * cos + swapped * sin_signed


def _proj_fox_kernel(x_ref, sc_ref, sh_ref, wq_ref, wk_ref, wv_ref, gq_ref, gk_ref,
                     qb_ref, kb_ref, vb_ref, k_ref, v_ref):
    h = _modulated(x_ref[...], sc_ref, sh_ref).astype(BF16)
    q = _rms_heads(_dot(h, wq_ref[...]), gq_ref[...], wq_ref.shape[1] // HEAD_DIM)
    for hh, blk in enumerate(q):
        qb_ref[:, hh * HEAD_DIM:(hh + 1) * HEAD_DIM] = blk.astype(BF16)
    k = _rms_heads(_dot(h, wk_ref[...]), gk_ref[...], wk_ref.shape[1] // HEAD_DIM)
    for hh, blk in enumerate(k):
        sl = slice(hh * HEAD_DIM, (hh + 1) * HEAD_DIM)
        k_ref[:, sl] = blk
        kb_ref[:, sl] = blk.astype(BF16)
    v = _dot(h, wv_ref[...])
    v_ref[...] = v
    vb_ref[...] = v.astype(BF16)


def _proj_dsa_kernel(x_ref, sc_ref, sh_ref, wq_ref, wk_ref, wv_ref, wi_ref, wm_ref,
                     gq_ref, gk_ref, bf_ref, c128_ref, s128_ref, c64_ref, s64_ref,
                     qb_ref, kb_ref, vb_ref, qi_ref, kia_ref, kib_ref, k_ref, v_ref, misc_ref):
    h = _modulated(x_ref[...], sc_ref, sh_ref).astype(BF16)
    c128, s128 = c128_ref[...], s128_ref[...]
    c64, s64 = c64_ref[...], s64_ref[...]
    lane = lax.broadcasted_iota(I32, c64.shape, 1)
    first_half = (lane % D_IDX) < (D_IDX // 2)

    q = _rms_heads(_dot(h, wq_ref[...]), gq_ref[...], wq_ref.shape[1] // HEAD_DIM)
    for hh, blk in enumerate(q):
        qb_ref[:, hh * HEAD_DIM:(hh + 1) * HEAD_DIM] = _rope128(blk, c128, s128).astype(BF16)
    k = _rms_heads(_dot(h, wk_ref[...]), gk_ref[...], wk_ref.shape[1] // HEAD_DIM)
    for hh, blk in enumerate(k):
        sl = slice(hh * HEAD_DIM, (hh + 1) * HEAD_DIM)
        r = _rope128(blk, c128, s128)
        k_ref[:, sl] = r
        kb_ref[:, sl] = r.astype(BF16)
    v = _dot(h, wv_ref[...])
    v_ref[...] = v
    vb_ref[...] = v.astype(BF16)

    qi = _dot(h, wi_ref[...])
    for p in range(wi_ref.shape[1] // LANES):
        sl = slice(p * LANES, (p + 1) * LANES)
        qi_ref[:, sl] = _rope64(qi[:, sl], c64, s64, first_half).astype(BF16)

    m = _dot(h, wm_ref[...])
    roped = _rope64(m, c64, s64, first_half)
    xf = m + bf_ref[...]
    logsig = jnp.minimum(xf, 0.0) - jnp.log1p(jnp.exp(-jnp.abs(xf)))
    wsc = m * (H_IDX ** -0.5)
    ki_only = jnp.where(lane < MISC_F, roped, 0.0)
    misc_ref[...] = jnp.where(lane < MISC_F, roped,
                              jnp.where(lane < MISC_W, logsig,
                                        jnp.where(lane < MISC_END, wsc, 0.0)))
    kia_ref[...] = ki_only.astype(BF16)
    kib_ref[...] = pltpu.roll(ki_only, D_IDX, axis=1).astype(BF16)


def _row_specs(tm, d, rows_per_group, mod_rows):
    tiles_per_group = rows_per_group // tm
    x_spec = pl.BlockSpec((tm, d), lambda i: (i, 0))
    mod_spec = pl.BlockSpec((None, mod_rows, d), lambda i: (i // tiles_per_group, 0, 0))
    return x_spec, mod_spec


def _full(shape):
    nd = len(shape)
    return pl.BlockSpec(shape, lambda i: (0,) * nd)


def _proj_fox(x2d, sc, sh, wq, wk, wv, gq, gk, tm, rows_per_group):
    m, d = x2d.shape
    x_spec, mod_spec = _row_specs(tm, d, rows_per_group, sc.shape[1])
    n = wq.shape[1]
    o_spec = pl.BlockSpec((tm, n), lambda i: (i, 0))
    return pl.pallas_call(
        _proj_fox_kernel,
        grid=(m // tm,),
        in_specs=[x_spec, mod_spec, mod_spec, _full(wq.shape), _full(wk.shape), _full(wv.shape),
                  _full(gq.shape), _full(gk.shape)],
        out_specs=[o_spec] * 5,
        out_shape=[jax.ShapeDtypeStruct((m, n), BF16)] * 3 + [jax.ShapeDtypeStruct((m, n), F32)] * 2,
        compiler_params=_cp(("parallel",)),
        name="proj_fox",
    )(x2d, sc, sh, wq, wk, wv, gq, gk)


def _proj_dsa(x2d, sc, sh, wq, wk, wv, wi, wm, gq, gk, bf, tabs, tm, rows_per_group, tiles_per_seq):
    m, d = x2d.shape
    x_spec, mod_spec = _row_specs(tm, d, rows_per_group, sc.shape[1])
    tab_spec = pl.BlockSpec((tm, LANES), lambda i: (i % tiles_per_seq, 0))
    nq, nk, ni = wq.shape[1], wk.shape[1], wi.shape[1]

    def ospec(n):
        return pl.BlockSpec((tm, n), lambda i: (i, 0))

    def oshape(n, dt):
        return jax.ShapeDtypeStruct((m, n), dt)

    return pl.pallas_call(
        _proj_dsa_kernel,
        grid=(m // tm,),
        in_specs=[x_spec, mod_spec, mod_spec, _full(wq.shape), _full(wk.shape), _full(wv.shape),
                  _full(wi.shape), _full(wm.shape), _full(gq.shape), _full(gk.shape), _full(bf.shape),
                  tab_spec, tab_spec, tab_spec, tab_spec],
        out_specs=[ospec(nq), ospec(nk), ospec(nk), ospec(ni), ospec(LANES), ospec(LANES),
                   ospec(nk), ospec(nk), ospec(LANES)],
        out_shape=[oshape(nq, BF16), oshape(nk, BF16), oshape(nk, BF16), oshape(ni, BF16),
                   oshape(LANES, BF16), oshape(LANES, BF16),
                   oshape(nk, F32), oshape(nk, F32), oshape(LANES, F32)],
        compiler_params=_cp(("parallel",)),
        name="proj_dsa",
    )(x2d, sc, sh, wq, wk, wv, wi, wm, gq, gk, bf, *tabs)


def _cumsum_kernel(x_ref, o_ref):
    x = x_ref[...]
    n = x.shape[1]
    lane = lax.broadcasted_iota(I32, x.shape, 1)
    k = 1
    while k < n:
        x = x + jnp.where(lane >= k, pltpu.roll(x, k, axis=1), 0.0)
        k *= 2
    o_ref[...] = x


def _cumsum_lanes(x):
    return pl.pallas_call(
        _cumsum_kernel,
        out_shape=jax.ShapeDtypeStruct(x.shape, F32),
        name="logf_cumsum",
    )(x)


def _fox_flash_kernel(q_ref, k_ref, v_ref, ck_ref, o_ref, s_ref, *, t, n_q):
    qi = pl.program_id(2)
    q = (q_ref[...].astype(F32) * (HEAD_DIM ** -0.5)).astype(BF16)

    def lane_fold(x, op, init):
        for j in range(x.shape[1] // LANES):
            init = op(init, x[:, j * LANES:(j + 1) * LANES])
        return init

    def attend(n_tiles):
        mx = jnp.full((t, LANES), -jnp.inf, F32)
        for c in range(n_tiles):
            s = _dot_nt(q, k_ref[c * t:(c + 1) * t, :]) - ck_ref[c]
            if c == n_tiles - 1:
                s = jnp.where(lax.broadcasted_iota(I32, s.shape, 1) <= lax.broadcasted_iota(I32, s.shape, 0),
                              s, -jnp.inf)
            s_ref[:, c * t:(c + 1) * t] = s
            mx = lane_fold(s, jnp.maximum, mx)
        m = jnp.max(mx, axis=-1, keepdims=True)
        ls = jnp.zeros((t, LANES), F32)
        acc = jnp.zeros((t, HEAD_DIM), F32)
        for c in range(n_tiles):
            p = jnp.exp(s_ref[:, c * t:(c + 1) * t] - m)
            ls = lane_fold(p, jnp.add, ls)
            acc = acc + _dot(p.astype(BF16), v_ref[c * t:(c + 1) * t, :])
        o_ref[...] = (acc / jnp.sum(ls, axis=-1, keepdims=True)).astype(o_ref.dtype)

    for n_tiles in range(1, n_q + 1):
        pl.when(qi == n_tiles - 1)(functools.partial(attend, n_tiles))


def _fox_prompt(qb, kb, vb, cum, t=512):
    b, s, hd = qb.shape
    h = hd // HEAD_DIM
    ck = cum.reshape(b, h, s // t, 1, t)
    kv_spec = pl.BlockSpec((None, s, HEAD_DIM), lambda bi, hi, qi: (bi, 0, hi))
    q_spec = pl.BlockSpec((None, t, HEAD_DIM), lambda bi, hi, qi: (bi, qi, hi))
    return pl.pallas_call(
        functools.partial(_fox_flash_kernel, t=t, n_q=s // t),
        grid=(b, h, s // t),
        in_specs=[q_spec, kv_spec, kv_spec,
                  pl.BlockSpec((None, None, s // t, 1, t), lambda bi, hi, qi: (bi, hi, 0, 0, 0))],
        out_specs=q_spec,
        out_shape=jax.ShapeDtypeStruct((b, s, hd), BF16),
        scratch_shapes=[pltpu.VMEM((t, s), F32)],
        compiler_params=_cp(("parallel", "parallel", "arbitrary")),
        name="fox_prompt",
    )(qb, kb, vb, ck)


def _order_key(x):
    bits = pltpu.bitcast(x + 0.0, I32)
    return jnp.where(bits < 0, bits ^ jnp.int32(0x7FFFFFFF), bits)


def _kth_largest_key(key, k, reduce_axes):
    shape = list(key.shape)
    for ax in reduce_axes:
        shape[ax] = 1
    sign = jnp.int32(-2 ** 31)

    def body(i, t):
        bit = lax.shift_left(jnp.int32(1), jnp.int32(31) - i)
        cand = t | bit
        ge = jnp.where(key >= (cand ^ sign), 1.0, 0.0)
        cnt = jnp.sum(ge, axis=reduce_axes, keepdims=True)
        return jnp.where(cnt >= float(k), cand, t)

    t = lax.fori_loop(0, 32, body, jnp.zeros(shape, I32))
    return t ^ sign


def _dsa_prompt_kernel(qi_ref, kia_ref, kib_ref, misc_ref, qd_ref, kd_ref, vd_ref, o_ref,
                       score_ref, selb_ref, *, tq, n_sel, q0):
    s_len = kia_ref.shape[0]
    row = (pl.program_id(1) + q0) * tq + lax.broadcasted_iota(I32, (tq, s_len), 0)
    col = lax.broadcasted_iota(I32, (tq, s_len), 1)
    causal = col <= row

    misc = misc_ref[...]
    kia, kib = kia_ref[...], kib_ref[...]
    acc = jnp.zeros((tq, s_len), F32)
    for p in range(qi_ref.shape[1] // LANES):
        qblk = qi_ref[:, p * LANES:(p + 1) * LANES]
        for half, kmat in enumerate((kia, kib)):
            hh = 2 * p + half
            w = misc[:, MISC_W + hh:MISC_W + hh + 1] * (D_IDX ** -0.5)
            acc = acc + w * jnp.maximum(_dot_nt(qblk, kmat), 0.0)
    score_ref[...] = jnp.where(causal, acc, -jnp.inf)

    key = _order_key(score_ref[...])
    thr = _kth_largest_key(key, n_sel, (1,))
    cnt_gt = jnp.sum(jnp.where(key > thr, 1.0, 0.0), axis=1, keepdims=True)
    cnt_eq = jnp.sum(jnp.where(key == thr, 1.0, 0.0), axis=1, keepdims=True)
    need = float(n_sel) - cnt_gt
    selb_ref[...] = jnp.where((key >= thr) & causal, 0.0, -jnp.inf)
    tie_rows = jnp.where((cnt_eq > need) & (thr > KEY_NEG_INF), 1.0, 0.0)

    @pl.when(jnp.max(tie_rows) > 0.0)
    def _():
        ch = 256
        r_i = lax.broadcasted_iota(I32, (ch, ch), 0)
        c_i = lax.broadcasted_iota(I32, (ch, ch), 1)
        tri = jnp.where(r_i < c_i, 1.0, 0.0).astype(BF16)
        base = jnp.zeros((tq, 1), F32)
        for c in range(s_len // ch):
            sl = slice(c * ch, (c + 1) * ch)
            eq_c = jnp.where(key[:, sl] == thr, 1.0, 0.0)
            rank = base + _dot(eq_c.astype(BF16), tri)
            take = (key[:, sl] > thr) | ((eq_c > 0.0) & (rank < need))
            selb_ref[:, sl] = jnp.where(take & causal[:, sl], 0.0, -jnp.inf)
            base = base + jnp.sum(eq_c, axis=1, keepdims=True)

    n_heads = qd_ref.shape[1] // HEAD_DIM
    rep = n_heads // KV_DSA
    for hh in range(n_heads):
        g = hh // rep
        q = qd_ref[:, hh * HEAD_DIM:(hh + 1) * HEAD_DIM]
        kg = kd_ref[:, g * HEAD_DIM:(g + 1) * HEAD_DIM]
        vg = vd_ref[:, g * HEAD_DIM:(g + 1) * HEAD_DIM]
        s = _dot_nt(q, kg) * (HEAD_DIM ** -0.5) + selb_ref[...]
        m = jnp.max(s, axis=-1, keepdims=True)
        p = jnp.exp(s - m)
        l = jnp.sum(p, axis=-1, keepdims=True)
        o_ref[:, hh * HEAD_DIM:(hh + 1) * HEAD_DIM] = (_dot(p.astype(BF16), vg) / l).astype(o_ref.dtype)


def _dsa_prompt(qi, kia, kib, misc, qd, kd, vd, tq=256, tiles_per_call=2):
    b, s, _ = qi.shape
    n_sel = min(TOPK_MAX, s // 4)
    outs = []
    for q0 in range(0, s // tq, tiles_per_call):
        s_eff = (q0 + tiles_per_call) * tq

        def qspec(n, q0=q0):
            return pl.BlockSpec((None, tq, n), lambda bi, i: (bi, i + q0, 0))

        def kspec(n, s_eff=s_eff):
            return pl.BlockSpec((None, s_eff, n), lambda bi, i: (bi, 0, 0))

        outs.append(pl.pallas_call(
            functools.partial(_dsa_prompt_kernel, tq=tq, n_sel=n_sel, q0=q0),
            grid=(b, tiles_per_call),
            in_specs=[qspec(qi.shape[2]), kspec(LANES), kspec(LANES), qspec(LANES),
                      qspec(qd.shape[2]), kspec(kd.shape[2]), kspec(vd.shape[2])],
            out_specs=pl.BlockSpec((None, tq, qd.shape[2]), lambda bi, i: (bi, i, 0)),
            out_shape=jax.ShapeDtypeStruct((b, tiles_per_call * tq, qd.shape[2]), BF16),
            scratch_shapes=[pltpu.VMEM((tq, s_eff), F32), pltpu.VMEM((tq, s_eff), F32)],
            compiler_params=_cp(("parallel", "parallel")),
            name="dsa_prompt",
        )(qi, kia, kib, misc, qd, kd, vd))
    return jnp.concatenate(outs, axis=1)


def _idx_score_rows(qib, w, kt_refs):
    rows = []
    for kt_ref in kt_refs:
        d = _dot(qib, kt_ref[...].astype(BF16))
        rows.append(jnp.sum(w * jnp.maximum(d, 0.0), axis=0, keepdims=True))
    return jnp.concatenate(rows, axis=0)


def _fox_sample_kernel(pt_ref, q_ref, kn_ref, vn_ref, lfn_ref, qi_ref, wi_ref, kin_ref, *refs, pp, n_heads):
    k_refs = refs[:pp]
    v_refs = refs[pp:2 * pp]
    lf_refs = refs[2 * pp:3 * pp]
    kt_refs = refs[3 * pp:4 * pp]
    o_ref, sc_ref, scn_ref, m_ref, l_ref, acc_ref, carry_ref = refs[4 * pp:]
    c = pl.program_id(1)
    width = PAGE_SIZE * n_heads
    q = q_ref[...] * (HEAD_DIM ** -0.5)
    qb = q.astype(BF16)
    qi = qi_ref[...]
    wi = wi_ref[...] * (D_IDX ** -0.5)
    own_head = (lax.broadcasted_iota(I32, (n_heads, width), 1) % n_heads
                == lax.broadcasted_iota(I32, (n_heads, width), 0))

    @pl.when(c == 0)
    def _():
        s_new = jnp.sum(q * kn_ref[...], axis=-1, keepdims=True)
        m_ref[...] = jnp.broadcast_to(s_new, m_ref.shape)
        l_ref[...] = jnp.ones(l_ref.shape, F32)
        acc_ref[...] = vn_ref[...]
        carry_ref[...] = lfn_ref[...]
        d = jnp.sum(qi * kin_ref[...], axis=-1, keepdims=True)
        sn = jnp.sum(wi * jnp.maximum(d, 0.0), axis=0, keepdims=True)
        r = lax.broadcasted_iota(I32, scn_ref.shape, 0)
        ln = lax.broadcasted_iota(I32, scn_ref.shape, 1)
        scn_ref[...] = jnp.where((r == 0) & (ln == 0), jnp.broadcast_to(sn, scn_ref.shape), -jnp.inf)

    sc_ref[...] = _idx_score_rows(qi.astype(BF16), wi, kt_refs)

    lf = jnp.concatenate([r[...] for r in lf_refs], axis=0)
    lane = lax.broadcasted_iota(I32, lf.shape, 1)
    tot, suf = lf, lf
    k = n_heads
    while k < width:
        tot = tot + pltpu.roll(tot, k, axis=1)
        suf = suf + jnp.where(lane + k < width, pltpu.roll(suf, width - k, axis=1), 0.0)
        k *= 2
    later = suf - lf
    run = carry_ref[...]
    bias = [None] * pp
    for j in reversed(range(pp)):
        bias[j] = run + later[j:j + 1, :]
        run = run + tot[j:j + 1, :]
    carry_ref[...] = run

    logits = []
    for j in range(pp):
        s = _dot_nt(qb, k_refs[j][...].astype(BF16))
        logits.append(jnp.where(own_head, s + bias[j], NEG))
    m_prev = m_ref[...]
    m_new = m_prev
    for j in range(pp):
        m_new = jnp.maximum(m_new, jnp.max(logits[j], axis=-1, keepdims=True))
    alpha = jnp.exp(m_prev - m_new)
    l_new = alpha * l_ref[...]
    acc = alpha * acc_ref[...]
    for j in range(pp):
        p = jnp.exp(logits[j] - m_new[:, :1])
        l_new = l_new + jnp.sum(p, axis=-1, keepdims=True)
        acc = acc + _dot(p.astype(BF16), v_refs[j][...].astype(BF16))
    m_ref[...] = m_new
    l_ref[...] = l_new
    acc_ref[...] = acc

    @pl.when(c == pl.num_programs(1) - 1)
    def _():
        o_ref[...] = acc / l_new


def _fox_sample(page_table, q, k_new, v_new, lf_new, qi, wi, ki_new, cache_k, cache_v, cache_lf, cache_idx_t, pp=8):
    db, n_heads, _ = q.shape
    n_pages = page_table.shape[1]
    nc = n_pages // pp
    width = PAGE_SIZE * n_heads

    def page_idx(j):
        return lambda bi, ci, pt: (pt[bi, (nc - 1 - ci) * pp + j], 0, 0)

    def per_seq(shape):
        return pl.BlockSpec((None,) + shape, lambda bi, ci, pt: (bi, 0, 0))

    vec_spec = per_seq((n_heads, HEAD_DIM))
    kv_specs = [pl.BlockSpec((None, width, HEAD_DIM), page_idx(j)) for j in range(pp)]
    lf_specs = [pl.BlockSpec((None, 1, width), page_idx(j)) for j in range(pp)]
    kt_specs = [pl.BlockSpec((None, D_IDX, PAGE_SIZE), page_idx(j)) for j in range(pp)]
    grid_spec = pltpu.PrefetchScalarGridSpec(
        num_scalar_prefetch=1,
        grid=(db, nc),
        in_specs=[vec_spec, vec_spec, vec_spec, per_seq((1, width)),
                  per_seq((H_IDX, D_IDX)), per_seq((H_IDX, LANES)), per_seq((1, D_IDX))]
                 + kv_specs + kv_specs + lf_specs + kt_specs,
        out_specs=[vec_spec,
                   pl.BlockSpec((None, pp, PAGE_SIZE), lambda bi, ci, pt: (bi, nc - 1 - ci, 0)),
                   per_seq((8, LANES))],
        scratch_shapes=[pltpu.VMEM((n_heads, HEAD_DIM), F32)] * 3 + [pltpu.VMEM((1, width), F32)],
    )
    return pl.pallas_call(
        functools.partial(_fox_sample_kernel, pp=pp, n_heads=n_heads),
        grid_spec=grid_spec,
        out_shape=[jax.ShapeDtypeStruct((db, n_heads, HEAD_DIM), F32),
                   jax.ShapeDtypeStruct((db, n_pages, PAGE_SIZE), F32),
                   jax.ShapeDtypeStruct((db, 8, LANES), F32)],
        compiler_params=_cp(("parallel", "arbitrary")),
        name="fox_sample",
    )(page_table, q, k_new, v_new, lf_new, qi, wi, ki_new,
      *([cache_k] * pp), *([cache_v] * pp), *([cache_lf] * pp), *([cache_idx_t] * pp))


def _select_kernel(s_ref, past_ref, new_ref, take_ref, *, n_sel, n_pages):
    score = s_ref[...]
    db, rows, _ = score.shape
    key = _order_key(score)
    thr = _kth_largest_key(key, n_sel, (1, 2))
    valid = score > -jnp.inf
    take_ref[...] = jnp.where((key >= thr) & valid, 1.0, 0.0)
    cnt_gt = jnp.sum(jnp.where(key > thr, 1.0, 0.0), axis=(1, 2), keepdims=True)
    cnt_eq = jnp.sum(jnp.where(key == thr, 1.0, 0.0), axis=(1, 2), keepdims=True)
    need = float(n_sel) - cnt_gt
    ties = jnp.where((cnt_eq > need) & (thr > KEY_NEG_INF), 1.0, 0.0)

    @pl.when(jnp.max(ties) > 0.0)
    def _():
        eq = jnp.where(key == thr, 1.0, 0.0)
        r_i = lax.broadcasted_iota(I32, (LANES, LANES), 0)
        c_i = lax.broadcasted_iota(I32, (LANES, LANES), 1)
        tri = jnp.where(r_i < c_i, 1.0, 0.0).astype(BF16)
        within = _dot(eq.reshape(db * rows, LANES).astype(BF16), tri).reshape(db, rows, LANES)
        row_cnt = jnp.broadcast_to(jnp.sum(eq, axis=2, keepdims=True), score.shape)
        rows_pad = 2 * LANES
        cnt_pad = jnp.concatenate([row_cnt, jnp.zeros((db, rows_pad - rows, LANES), F32)], axis=1)
        rr = lax.broadcasted_iota(I32, (db, rows_pad, rows_pad), 1)
        rc = lax.broadcasted_iota(I32, (db, rows_pad, rows_pad), 2)
        before = jnp.einsum("bij,bjl->bil", jnp.where(rc < rr, 1.0, 0.0).astype(BF16), cnt_pad.astype(BF16),
                            preferred_element_type=F32)[:, :rows, :]
        take = (key > thr) | ((eq > 0.0) & (within + before < need))
        take_ref[...] = jnp.where(take & valid, 1.0, 0.0)

    take01 = take_ref[...]
    new_ref[...] = jnp.where(take01[:, n_pages:, :] > 0.5, 0.0, NEG)
    t_i = lax.broadcasted_iota(I32, (LANES, LANES * KV_DSA), 0)
    l_i = lax.broadcasted_iota(I32, (LANES, LANES * KV_DSA), 1)
    spread = jnp.where(l_i // KV_DSA == t_i, 1.0, 0.0).astype(BF16)
    past = _dot(take01[:, :n_pages, :].reshape(db * n_pages, LANES).astype(BF16), spread)
    past_ref[...] = jnp.where(past > 0.5, 0.0, NEG).reshape(db, n_pages, LANES * KV_DSA)


def _select_sample(scores, n_sel, n_pages):
    db, rows, lanes = scores.shape
    return pl.pallas_call(
        functools.partial(_select_kernel, n_sel=n_sel, n_pages=n_pages),
        out_shape=[jax.ShapeDtypeStruct((db, n_pages, lanes * KV_DSA), F32),
                   jax.ShapeDtypeStruct((db, rows - n_pages, lanes), F32)],
        scratch_shapes=[pltpu.VMEM((db, rows, lanes), F32)],
        compiler_params=pltpu.CompilerParams(vmem_limit_bytes=VMEM_LIMIT),
        name="select_sample",
    )(scores)


def _dsa_sample_kernel(pt_ref, q_ref, kn_ref, vn_ref, selb_ref, selbn_ref, *refs, pp, n_heads):
    k_refs = refs[:pp]
    v_refs = refs[pp:2 * pp]
    o_ref, m_ref, l_ref, acc_ref = refs[2 * pp:]
    c = pl.program_id(1)
    rep = n_heads // KV_DSA
    width = PAGE_SIZE * KV_DSA
    q = q_ref[...] * (HEAD_DIM ** -0.5)
    qb = q.astype(BF16)
    grp = lax.broadcasted_iota(I32, (n_heads, HEAD_DIM), 0) // rep
    own_kv = (lax.broadcasted_iota(I32, (n_heads, width), 1) % KV_DSA
              == lax.broadcasted_iota(I32, (n_heads, width), 0) // rep)

    @pl.when(c == 0)
    def _():
        kn = kn_ref[...]
        vn = vn_ref[...]
        s_new = jnp.zeros((n_heads, 1), F32)
        v_rows = jnp.zeros((n_heads, HEAD_DIM), F32)
        for g in range(KV_DSA):
            sg = jnp.sum(q * kn[g:g + 1, :], axis=-1, keepdims=True)
            s_new = jnp.where(grp[:, :1] == g, sg, s_new)
            v_rows = jnp.where(grp == g, jnp.broadcast_to(vn[g:g + 1, :], v_rows.shape), v_rows)
        s_new = s_new + selbn_ref[0:1, 0:1]
        m_ref[...] = jnp.broadcast_to(jnp.maximum(s_new, NEG), m_ref.shape)
        p_new = jnp.exp(s_new - m_ref[:, :1])
        l_ref[...] = jnp.broadcast_to(p_new, l_ref.shape)
        acc_ref[...] = p_new * v_rows

    selb = selb_ref[...]
    logits = []
    for j in range(pp):
        s = _dot_nt(qb, k_refs[j][...].astype(BF16))
        logits.append(jnp.where(own_kv, s + selb[j:j + 1, :], NEG))

    m_prev = m_ref[...]
    m_new = m_prev
    for j in range(pp):
        m_new = jnp.maximum(m_new, jnp.max(logits[j], axis=-1, keepdims=True))
    alpha = jnp.exp(m_prev - m_new)
    l_new = alpha * l_ref[...]
    acc = alpha * acc_ref[...]
    for j in range(pp):
        p = jnp.exp(logits[j] - m_new[:, :1])
        l_new = l_new + jnp.sum(p, axis=-1, keepdims=True)
        acc = acc + _dot(p.astype(BF16), v_refs[j][...].astype(BF16))
    m_ref[...] = m_new
    l_ref[...] = l_new
    acc_ref[...] = acc

    @pl.when(c == pl.num_programs(1) - 1)
    def _():
        o_ref[...] = acc / l_new


def _dsa_sample(page_table, q, k_new, v_new, selb_past, selb_new, cache_k, cache_v, pp=16):
    db, n_heads, _ = q.shape
    n_pages = page_table.shape[1]
    nc = n_pages // pp
    width = PAGE_SIZE * KV_DSA

    def page_idx(j):
        return lambda bi, ci, pt: (pt[bi, ci * pp + j], 0, 0)

    q_spec = pl.BlockSpec((None, n_heads, HEAD_DIM), lambda bi, ci, pt: (bi, 0, 0))
    n_spec = pl.BlockSpec((None, KV_DSA, HEAD_DIM), lambda bi, ci, pt: (bi, 0, 0))
    kv_specs = [pl.BlockSpec((None, width, HEAD_DIM), page_idx(j)) for j in range(pp)]
    grid_spec = pltpu.PrefetchScalarGridSpec(
        num_scalar_prefetch=1,
        grid=(db, nc),
        in_specs=[q_spec, n_spec, n_spec,
                  pl.BlockSpec((None, pp, width), lambda bi, ci, pt: (bi, ci, 0)),
                  pl.BlockSpec((None,) + selb_new.shape[1:], lambda bi, ci, pt: (bi, 0, 0))]
                 + kv_specs + kv_specs,
        out_specs=q_spec,
        scratch_shapes=[pltpu.VMEM((n_heads, HEAD_DIM), F32)] * 3,
    )
    return pl.pallas_call(
        functools.partial(_dsa_sample_kernel, pp=pp, n_heads=n_heads),
        grid_spec=grid_spec,
        out_shape=jax.ShapeDtypeStruct((db, n_heads, HEAD_DIM), F32),
        compiler_params=_cp(("parallel", "arbitrary")),
        name="dsa_sample",
    )(page_table, q, k_new, v_new, selb_past, selb_new, *([cache_k] * pp), *([cache_v] * pp))


def _outproj_kernel(*refs, aliased, n_real):
    x1_ref, h2_ref, route_ref = refs[9 + aliased:]

    @pl.when(pl.program_id(0) < n_real)
    def _():
        _outproj_body(*refs[:9], x1_ref, h2_ref, route_ref)

    @pl.when(pl.program_id(0) >= n_real)
    def _():
        h2_ref[...] = jnp.zeros(h2_ref.shape, F32)
        route_ref[...] = jnp.zeros(route_ref.shape, F32)


def _outproj_body(x_ref, of_ref, od_ref, wf_ref, wd_ref, ga_ref, sc_ref, sh_ref, wr_ref,
                  x1_ref, h2_ref, route_ref):
    mix = _dot(of_ref[...].astype(BF16), wf_ref[...]) + _dot(od_ref[...].astype(BF16), wd_ref[...])
    x1 = x_ref[...] + ga_ref[...] * mix
    x1_ref[...] = x1
    h2 = _modulated(x1, sc_ref, sh_ref)
    h2_ref[...] = h2
    logits = _dot(h2.astype(BF16), wr_ref[...])
    lane = lax.broadcasted_iota(I32, logits.shape, 1).astype(F32)

    def first_argmax(vals, mask):
        v = jnp.where(mask, vals, -jnp.inf)
        mx = jnp.max(v, axis=-1, keepdims=True)
        idx = jnp.min(jnp.where(v == mx, lane, float(LANES)), axis=-1, keepdims=True)
        return mx, idx

    is_g = lane < float(N_GROUPS)
    mg, g_sel = first_argmax(logits, is_g)
    p_gsel = 1.0 / jnp.sum(jnp.where(is_g, jnp.exp(logits - mg), 0.0), axis=-1, keepdims=True)
    lo = float(N_GROUPS) + g_sel * float(EXPERTS_PER_GROUP)
    in_grp = (lane >= lo) & (lane < lo + float(EXPERTS_PER_GROUP))
    m1, i1 = first_argmax(logits, in_grp)
    m2, i2 = first_argmax(logits, in_grp & (lane != i1))
    e2 = jnp.exp(m2 - m1)
    w1 = p_gsel / (1.0 + e2)
    w2 = p_gsel * e2 / (1.0 + e2)
    route_ref[...] = jnp.where(lane == 0.0, i1 - float(N_GROUPS),
                               jnp.where(lane == 1.0, i2 - float(N_GROUPS),
                                         jnp.where(lane == 2.0, w1, jnp.where(lane == 3.0, w2, 0.0))))


def _outproj(x2d, o_fox, o_dsa, wf, wd, ga, sc, sh, wr, tm, rows_per_group, total_rows, row_offset,
             prev=None):
    m, d = x2d.shape
    n_real = m // tm
    tiles_per_group = rows_per_group // tm
    off = row_offset // tm
    half = o_fox.shape[1]
    n_tail = 0 if prev is not None else (total_rows - row_offset - m) // tm

    def row_idx(i):
        return (jnp.minimum(i, n_real - 1), 0)

    def mod_idx(i):
        return (jnp.minimum(i, n_real - 1) // tiles_per_group, 0, 0)

    mod_spec = pl.BlockSpec((None, sc.shape[1], d), mod_idx)
    in_specs = [pl.BlockSpec((tm, d), row_idx), pl.BlockSpec((tm, half), row_idx), pl.BlockSpec((tm, half), row_idx),
                _full(wf.shape), _full(wd.shape), mod_spec, mod_spec, mod_spec, _full(wr.shape)]
    args = [x2d, o_fox, o_dsa, wf, wd, ga, sc, sh, wr]
    aliases = {}
    if prev is not None:
        in_specs += [pl.BlockSpec(memory_space=pl.ANY), pl.BlockSpec(memory_space=pl.ANY)]
        args += list(prev)
        aliases = {9: 1, 10: 2}
    return pl.pallas_call(
        functools.partial(_outproj_kernel, aliased=0 if prev is None else 2, n_real=n_real),
        grid=(n_real + n_tail,),
        in_specs=in_specs,
        out_specs=[pl.BlockSpec((tm, d), row_idx),
                   pl.BlockSpec((tm, d), lambda i: (i + off, 0)),
                   pl.BlockSpec((tm, LANES), lambda i: (i + off, 0))],
        out_shape=[jax.ShapeDtypeStruct((m, d), F32),
                   jax.ShapeDtypeStruct((total_rows, d), F32),
                   jax.ShapeDtypeStruct((total_rows, LANES), F32)],
        input_output_aliases=aliases,
        compiler_params=_cp(("arbitrary",)),
        name="outproj_router",
    )(*args)


def _moe_kernel(te_ref, ta_ref, ids_ref, h_hbm, wg_ref, wu_ref, wd_ref, o_ref, hbuf, sem, *, tm):
    t = pl.program_id(0)

    def row_copy(r):
        return pltpu.make_async_copy(h_hbm.at[pl.ds(ids_ref[0, r], 1)], hbuf.at[pl.ds(r, 1)], sem)

    @pl.when(ta_ref[t] > 0)
    def _():
        def issue(r, carry):
            row_copy(r).start()
            return carry

        lax.fori_loop(0, tm, issue, 0, unroll=8)
        pltpu.make_async_copy(h_hbm.at[pl.ds(0, tm)], hbuf, sem).wait()
        h = hbuf[...].astype(BF16)
        a = _dot(h, wg_ref[...].astype(BF16))
        u = _dot(h, wu_ref[...].astype(BF16))
        act = (a / (1.0 + jnp.exp(-a))) * u
        o_ref[...] = _dot(act.astype(BF16), wd_ref[...].astype(BF16))

    @pl.when(ta_ref[t] == 0)
    def _():
        o_ref[...] = jnp.zeros(o_ref.shape, F32)


def _moe_experts(tile_expert, tile_active, slot_token, h_all, w_gate, w_up, w_down, tm):
    n_tiles = tile_expert.shape[0]
    d = h_all.shape[1]
    f = w_gate.shape[2]
    grid_spec = pltpu.PrefetchScalarGridSpec(
        num_scalar_prefetch=2,
        grid=(n_tiles,),
        in_specs=[pl.BlockSpec((None, 1, tm), lambda t, te, ta: (t, 0, 0), memory_space=pltpu.SMEM),
                  pl.BlockSpec(memory_space=pl.ANY),
                  pl.BlockSpec((None, d, f), lambda t, te, ta: (te[t], 0, 0)),
                  pl.BlockSpec((None, d, f), lambda t, te, ta: (te[t], 0, 0)),
                  pl.BlockSpec((None, f, d), lambda t, te, ta: (te[t], 0, 0))],
        out_specs=pl.BlockSpec((tm, d), lambda t, te, ta: (t, 0)),
        scratch_shapes=[pltpu.VMEM((tm, d), F32), pltpu.SemaphoreType.DMA(())],
    )
    return pl.pallas_call(
        functools.partial(_moe_kernel, tm=tm),
        grid_spec=grid_spec,
        out_shape=jax.ShapeDtypeStruct((n_tiles * tm, d), F32),
        compiler_params=_cp(("arbitrary",)),
        name="moe_experts",
    )(tile_expert, tile_active, slot_token.reshape(n_tiles, 1, tm), h_all, w_gate, w_up, w_down)


def _combine_kernel(s0_ref, s1_ref, x1_ref, gm_ref, route_ref, y_hbm, o_ref, b0, b1, sem, *, tm):
    def copies(r):
        return (pltpu.make_async_copy(y_hbm.at[pl.ds(s0_ref[0, r], 1)], b0.at[pl.ds(r, 1)], sem),
                pltpu.make_async_copy(y_hbm.at[pl.ds(s1_ref[0, r], 1)], b1.at[pl.ds(r, 1)], sem))

    def issue(r, carry):
        c0, c1 = copies(r)
        c0.start()
        c1.start()
        return carry

    lax.fori_loop(0, tm, issue, 0, unroll=8)
    pltpu.make_async_copy(y_hbm.at[pl.ds(0, tm)], b0, sem).wait()
    pltpu.make_async_copy(y_hbm.at[pl.ds(0, tm)], b1, sem).wait()
    route = route_ref[...]
    moe = route[:, 2:3] * b0[...] + route[:, 3:4] * b1[...]
    o_ref[...] = x1_ref[...] + gm_ref[...] * moe


def _combine(slot0, slot1, x1, gm, route_all, y_slots, tm, rows_per_group, row_offset):
    m, d = x1.shape
    x_spec, mod_spec = _row_specs(tm, d, rows_per_group, gm.shape[1])
    off = row_offset // tm
    n_tiles = m // tm
    id_spec = pl.BlockSpec((None, 1, tm), lambda i: (i, 0, 0), memory_space=pltpu.SMEM)
    return pl.pallas_call(
        functools.partial(_combine_kernel, tm=tm),
        grid=(n_tiles,),
        in_specs=[id_spec, id_spec, x_spec, mod_spec,
                  pl.BlockSpec((tm, LANES), lambda i: (i + off, 0)),
                  pl.BlockSpec(memory_space=pl.ANY)],
        out_specs=x_spec,
        out_shape=jax.ShapeDtypeStruct((m, d), F32),
        scratch_shapes=[pltpu.VMEM((tm, d), F32), pltpu.VMEM((tm, d), F32), pltpu.SemaphoreType.DMA(())],
        compiler_params=_cp(("arbitrary",)),
        name="moe_combine",
    )(slot0.reshape(n_tiles, 1, tm), slot1.reshape(n_tiles, 1, tm), x1, gm, route_all, y_slots)


def _moe_plan(route_all, tm):
    t_all = route_all.shape[0]
    eid = route_all[:, :2].astype(I32)
    flat = eid.reshape(-1)
    onehot = (flat[:, None] == jnp.arange(N_EXPERTS, dtype=I32)[None, :]).astype(I32)
    rank = jnp.sum((jnp.cumsum(onehot, axis=0) - onehot) * onehot, axis=1)
    counts = jnp.sum(onehot, axis=0)
    padded = ((counts + tm - 1) // tm) * tm
    ends = jnp.cumsum(padded)
    starts = ends - padded
    slot = starts[flat] + rank
    n_tiles = (2 * t_all + N_EXPERTS * (tm - 1)) // tm + 1
    tile_start = jnp.arange(n_tiles, dtype=I32) * tm
    tile_expert = jnp.minimum(jnp.sum((tile_start[:, None] >= ends[None, :]).astype(I32), axis=1),
                              N_EXPERTS - 1)
    tile_active = (tile_start < ends[-1]).astype(I32)
    token_of_pair = jnp.arange(2 * t_all, dtype=I32) // 2
    slot_token = jnp.zeros((n_tiles * tm,), I32).at[slot].set(token_of_pair)
    slots = slot.reshape(t_all, 2)
    return tile_expert.astype(I32), tile_active, slot_token, slots[:, 0], slots[:, 1]


def _rope_tables(pos, dim, lanes):
    half = dim // 2
    inv = ROPE_THETA ** (-jnp.arange(half, dtype=F32) / half)
    ang = pos.astype(F32)[:, None] * inv[None, :]
    cos, sin = jnp.cos(ang), jnp.sin(ang)
    reps = lanes // dim
    cos_t = jnp.tile(jnp.concatenate([cos, cos], axis=-1), (1, reps))
    sin_t = jnp.tile(jnp.concatenate([-sin, sin], axis=-1), (1, reps))
    return cos_t, sin_t


def kernel(x_prompt, x_sample, cache_fox_k, cache_fox_v, cache_fox_logf, cache_dsa_k, cache_dsa_v,
           cache_idx_k, page_table, c_prompt, c_sample, w_in, b_forget, q_gain_fox, k_gain_fox,
           q_gain_dsa, k_gain_dsa, w_out, w_ada, b_ada, w_router_group, w_router_expert,
           w_gate, w_up, w_down):
    b, s, d = x_prompt.shape
    db, dq, _ = x_sample.shape
    depth = w_in.shape[0]
    assert depth == 1 and dq == 1
    past_len = page_table.shape[1] * PAGE_SIZE
    n_phys = cache_fox_k.shape[1]
    h_fox = cache_fox_k.shape[3]
    h_dsa = d // (2 * HEAD_DIM)
    t_p, t_s = b * s, db * dq
    t_all = t_p + t_s
    tm_p, tm_s = 256, t_s
    layer = 0

    n_c = b + db
    pad = (-n_c) % 8
    c_all = jnp.concatenate([c_prompt, c_sample, jnp.zeros((pad, d), F32)], axis=0)
    mod = _adaln(c_all, w_ada[layer], b_ada[layer])
    mods = [mod[:, i * d:(i + 1) * d] for i in range(6)]
    mod_p = [m_[:b].reshape(b, 1, d) for m_ in mods]
    mod_s = [m_[b:b + db].reshape(1, db, d) for m_ in mods]

    w = w_in[layer]
    nf = h_fox * HEAD_DIM
    nd = h_dsa * HEAD_DIM
    nkv = KV_DSA * HEAD_DIM
    ni = H_IDX * D_IDX
    o = 0
    wqf = w[:, o:o + nf].astype(BF16); o += nf
    wkf = w[:, o:o + nf].astype(BF16); o += nf
    wvf = w[:, o:o + nf].astype(BF16); o += nf
    w_fl = w[:, o:o + h_fox]; o += h_fox
    wqd = w[:, o:o + nd].astype(BF16); o += nd
    wkd = w[:, o:o + nkv].astype(BF16); o += nkv
    wvd = w[:, o:o + nkv].astype(BF16); o += nkv
    wqi = w[:, o:o + ni].astype(BF16); o += ni
    w_ki = w[:, o:o + D_IDX]; o += D_IDX
    w_wi = w[:, o:o + H_IDX]; o += H_IDX
    assert h_fox <= MISC_W - MISC_F
    wmisc = jnp.concatenate([w_ki, w_fl, jnp.zeros((d, MISC_W - MISC_F - h_fox), F32), w_wi,
                             jnp.zeros((d, LANES - MISC_END), F32)], axis=1).astype(BF16)
    bf = jnp.zeros((1, LANES), F32).at[0, MISC_F:MISC_F + h_fox].set(b_forget[layer])
    gqf, gkf = q_gain_fox[layer].reshape(1, HEAD_DIM), k_gain_fox[layer].reshape(1, HEAD_DIM)
    gqd, gkd = q_gain_dsa[layer].reshape(1, HEAD_DIM), k_gain_dsa[layer].reshape(1, HEAD_DIM)

    pos_p = jnp.arange(s, dtype=I32)
    pos_s = jnp.full((t_s,), past_len, I32)
    tabs_p = _rope_tables(pos_p, HEAD_DIM, LANES) + _rope_tables(pos_p, D_IDX, LANES)
    tabs_s = _rope_tables(pos_s, HEAD_DIM, LANES) + _rope_tables(pos_s, D_IDX, LANES)

    xp2 = x_prompt.reshape(t_p, d)
    xs2 = x_sample.reshape(t_s, d)

    def project(x2d, mod_, tabs, tm, rows_per_group, tiles_per_seq):
        fox = _proj_fox(x2d, mod_[1], mod_[0], wqf, wkf, wvf, gqf, gkf, tm, rows_per_group)
        dsa = _proj_dsa(x2d, mod_[1], mod_[0], wqd, wkd, wvd, wqi, wmisc, gqd, gkd, bf, tabs,
                        tm, rows_per_group, tiles_per_seq)
        return fox, dsa

    (qf_p, kfb_p, vfb_p, kf_p, vf_p), (qd_p, kdb_p, vdb_p, qi_p, kia_p, kib_p, kd_p, vd_p, misc_p) = \
        project(xp2, mod_p, tabs_p, tm_p, s, s // tm_p)
    (qf_s, _, _, kf_s, vf_s), (qd_s, _, _, qi_s, _, _, kd_s, vd_s, misc_s) = \
        project(xs2, mod_s, tabs_s, tm_s, t_s, 1)

    lf_p = misc_p[:, MISC_F:MISC_F + h_fox].reshape(b, s, h_fox)
    cum = _cumsum_lanes(lf_p.transpose(0, 2, 1).reshape(b * h_fox, s)).reshape(b, h_fox, s)
    o_fox_p = _fox_prompt(qf_p.reshape(b, s, nf), kfb_p.reshape(b, s, nf), vfb_p.reshape(b, s, nf), cum)
    o_dsa_p = _dsa_prompt(qi_p.reshape(b, s, ni), kia_p.reshape(b, s, LANES), kib_p.reshape(b, s, LANES),
                          misc_p.reshape(b, s, LANES), qd_p.reshape(b, s, nd),
                          kdb_p.reshape(b, s, nkv), vdb_p.reshape(b, s, nkv))

    lf_s = misc_s[:, MISC_F:MISC_F + h_fox]
    ck = cache_fox_k[layer].reshape(n_phys, PAGE_SIZE * h_fox, HEAD_DIM)
    cv = cache_fox_v[layer].reshape(n_phys, PAGE_SIZE * h_fox, HEAD_DIM)
    clf = cache_fox_logf[layer].astype(F32).reshape(n_phys, 1, PAGE_SIZE * h_fox)
    w_s = misc_s[:, MISC_W:MISC_W + H_IDX]
    n_pages = page_table.shape[1]
    o_fox_s, sc_past, sc_new = _fox_sample(
        page_table,
        qf_s.astype(F32).reshape(db, h_fox, HEAD_DIM),
        kf_s.reshape(db, h_fox, HEAD_DIM), vf_s.reshape(db, h_fox, HEAD_DIM),
        jnp.tile(lf_s, (1, PAGE_SIZE)).reshape(db, 1, PAGE_SIZE * h_fox),
        qi_s.astype(F32).reshape(db, H_IDX, D_IDX),
        jnp.broadcast_to(w_s[:, :, None], (db, H_IDX, LANES)),
        misc_s[:, MISC_KI:MISC_KI + D_IDX].reshape(db, 1, D_IDX),
        ck, cv, clf, cache_idx_k[layer].transpose(0, 2, 1))
    n_sel = min(TOPK_MAX, (past_len + dq) // 4)
    selb_past, selb_new = _select_sample(jnp.concatenate([sc_past, sc_new], axis=1), n_sel, n_pages)
    cdk = cache_dsa_k[layer].reshape(n_phys, PAGE_SIZE * KV_DSA, HEAD_DIM)
    cdv = cache_dsa_v[layer].reshape(n_phys, PAGE_SIZE * KV_DSA, HEAD_DIM)
    o_dsa_s = _dsa_sample(page_table, qd_s.astype(F32).reshape(db, h_dsa, HEAD_DIM),
                          kd_s.reshape(db, KV_DSA, HEAD_DIM), vd_s.reshape(db, KV_DSA, HEAD_DIM),
                          selb_past, selb_new, cdk, cdv)

    wo = w_out[layer].astype(BF16)
    wof, wod = wo[:nf], wo[nf:]
    wr = jnp.concatenate([w_router_group[layer],
                          w_router_expert[layer].transpose(1, 0, 2).reshape(d, N_EXPERTS),
                          jnp.zeros((d, LANES - N_GROUPS - N_EXPERTS), F32)], axis=1).astype(BF16)
    rows_all = t_p + ((t_s + tm_p - 1) // tm_p) * tm_p
    x1_p, h2_all, route_all = _outproj(xp2, o_fox_p.reshape(t_p, nf), o_dsa_p.reshape(t_p, nd), wof, wod,
                                       mod_p[2], mod_p[4], mod_p[3], wr, tm_p, s, rows_all, 0)
    x1_s, h2_all, route_all = _outproj(xs2, o_fox_s.reshape(t_s, nf), o_dsa_s.reshape(t_s, nd), wof, wod,
                                       mod_s[2], mod_s[4], mod_s[3], wr, tm_s, t_s, rows_all, t_p,
                                       prev=(h2_all, route_all))

    tile_expert, tile_active, slot_token, slot0, slot1 = _moe_plan(route_all[:t_all], tm_p)
    y_slots = _moe_experts(tile_expert, tile_active, slot_token, h2_all,
                           w_gate[layer], w_up[layer], w_down[layer], tm_p)
    y_p = _combine(slot0[:t_p], slot1[:t_p], x1_p, mod_p[5], route_all, y_slots, tm_p, s, 0)
    y_s = _combine(slot0[t_p:], slot1[t_p:], x1_s, mod_s[5], route_all, y_slots, tm_s, t_s, t_p)

    def rows(kf, vf, misc, kd, vd, g, t):
        return (kf.reshape(1, g, t, h_fox, HEAD_DIM), vf.reshape(1, g, t, h_fox, HEAD_DIM),
                misc[:, MISC_F:MISC_F + h_fox].reshape(1, g, t, h_fox),
                kd.reshape(1, g, t, KV_DSA, HEAD_DIM), vd.reshape(1, g, t, KV_DSA, HEAD_DIM),
                misc[:, MISC_KI:MISC_KI + D_IDX].reshape(1, g, t, D_IDX))

    return ((y_p.reshape(b, s, d), y_s.reshape(db, dq, d))
            + rows(kf_p, vf_p, misc_p, kd_p, vd_p, b, s)
            + rows(kf_s, vf_s, misc_s, kd_s, vd_s, db, dq))
```

```python
import functools

import jax
import jax.numpy as jnp
from jax import lax
from jax.experimental import pallas as pl
from jax.experimental.pallas import tpu as pltpu

F32 = jnp.float32
BF16 = jnp.bfloat16
I32 = jnp.int32

HEAD_DIM = 128
D_IDX = 64
H_IDX = 16
KV_DSA = 2
TOPK_MAX = 256
N_GROUPS = 4
EXPERTS_PER_GROUP = 8
N_EXPERTS = N_GROUPS * EXPERTS_PER_GROUP
ROPE_THETA = 10000.0
EPS = 1e-6
PAGE_SIZE = 128

LANES = 128
NEG = -1e30
KEY_NEG_INF = -2139095041
BISECT_STEPS = 40
VMEM_LIMIT = 52 * 1024 * 1024

MISC_KI = 0
MISC_F = 64
MISC_W = 72
MISC_END = 88


def _cp(sem):
    return pltpu.CompilerParams(dimension_semantics=sem, vmem_limit_bytes=VMEM_LIMIT)


def _dot(a, b):
    return jnp.dot(a, b, preferred_element_type=F32)


def _dot_nt(a, b):
    return lax.dot_general(a, b, (((1,), (1,)), ((), ())), preferred_element_type=F32)


def _adaln_kernel(c_ref, w_ref, b_ref, o_ref):
    c = c_ref[...]
    s = c / (1.0 + jnp.exp(-c))
    o_ref[...] = _dot(s.astype(BF16), w_ref[...].astype(BF16)) + b_ref[...]


def _adaln(c_all, w_ada, b_ada):
    rows, d = c_all.shape
    n = w_ada.shape[1]
    tn = 1024
    return pl.pallas_call(
        _adaln_kernel,
        grid=(n // tn,),
        in_specs=[pl.BlockSpec((rows, d), lambda j: (0, 0)),
                  pl.BlockSpec((d, tn), lambda j: (0, j)),
                  pl.BlockSpec((1, tn), lambda j: (0, j))],
        out_specs=pl.BlockSpec((rows, tn), lambda j: (0, j)),
        out_shape=jax.ShapeDtypeStruct((rows, n), F32),
        compiler_params=_cp(("parallel",)),
        name="adaln",
    )(c_all, w_ada, b_ada.reshape(1, n))


def _modulated(x, sc_ref, sh_ref):
    ms = jnp.mean(x * x, axis=-1, keepdims=True)
    return (x * lax.rsqrt(ms + EPS)) * (1.0 + sc_ref[...]) + sh_ref[...]


def _rms_heads(acc, gain, n_heads):
    outs = []
    for hh in range(n_heads):
        blk = acc[:, hh * HEAD_DIM:(hh + 1) * HEAD_DIM]
        ms = jnp.mean(blk * blk, axis=-1, keepdims=True)
        outs.append(blk * lax.rsqrt(ms + EPS) * gain)
    return outs


def _rope128(y, cos, sin_signed):
    return y * cos + pltpu.roll(y, HEAD_DIM // 2, axis=1) * sin_signed


def _rope64(y, cos, sin_signed, first_half):
    swapped = jnp.where(first_half, pltpu.roll(y, LANES - D_IDX // 2, axis=1),
                        pltpu.roll(y, D_IDX // 2, axis=1))
    return y * cos + swapped * sin_signed


def _proj_fox_kernel(x_ref, sc_ref, sh_ref, wq_ref, wk_ref, wv_ref, gq_ref, gk_ref,
                     qb_ref, kb_ref, vb_ref, k_ref, v_ref):
    h = _modulated(x_ref[...], sc_ref, sh_ref).astype(BF16)
    q = _rms_heads(_dot(h, wq_ref[...]), gq_ref[...], wq_ref.shape[1] // HEAD_DIM)
    for hh, blk in enumerate(q):
        qb_ref[:, hh * HEAD_DIM:(hh + 1) * HEAD_DIM] = blk.astype(BF16)
    k = _rms_heads(_dot(h, wk_ref[...]), gk_ref[...], wk_ref.shape[1] // HEAD_DIM)
    for hh, blk in enumerate(k):
        sl = slice(hh * HEAD_DIM, (hh + 1) * HEAD_DIM)
        k_ref[:, sl] = blk
        kb_ref[:, sl] = blk.astype(BF16)
    v = _dot(h, wv_ref[...])
    v_ref[...] = v
    vb_ref[...] = v.astype(BF16)


def _proj_dsa_kernel(x_ref, sc_ref, sh_ref, wq_ref, wk_ref, wv_ref, wi_ref, wm_ref,
                     gq_ref, gk_ref, bf_ref, c128_ref, s128_ref, c64_ref, s64_ref,
                     qb_ref, kb_ref, vb_ref, qi_ref, kia_ref, kib_ref, k_ref, v_ref, misc_ref):
    hf = _modulated(x_ref[...], sc_ref, sh_ref)
    h = hf.astype(BF16)

    def idx_dot(w_ref):
        if w_ref.dtype == F32:
            return jnp.dot(hf, w_ref[...], precision=lax.Precision.HIGHEST, preferred_element_type=F32)
        return _dot(h, w_ref[...])

    c128, s128 = c128_ref[...], s128_ref[...]
    c64, s64 = c64_ref[...], s64_ref[...]
    lane = lax.broadcasted_iota(I32, c64.shape, 1)
    first_half = (lane % D_IDX) < (D_IDX // 2)

    q = _rms_heads(_dot(h, wq_ref[...]), gq_ref[...], wq_ref.shape[1] // HEAD_DIM)
    for hh, blk in enumerate(q):
        qb_ref[:, hh * HEAD_DIM:(hh + 1) * HEAD_DIM] = _rope128(blk, c128, s128).astype(BF16)
    k = _rms_heads(_dot(h, wk_ref[...]), gk_ref[...], wk_ref.shape[1] // HEAD_DIM)
    for hh, blk in enumerate(k):
        sl = slice(hh * HEAD_DIM, (hh + 1) * HEAD_DIM)
        r = _rope128(blk, c128, s128)
        k_ref[:, sl] = r
        kb_ref[:, sl] = r.astype(BF16)
    v = _dot(h, wv_ref[...])
    v_ref[...] = v
    vb_ref[...] = v.astype(BF16)

    qi = idx_dot(wi_ref)
    for p in range(wi_ref.shape[1] // LANES):
        sl = slice(p * LANES, (p + 1) * LANES)
        qi_ref[:, sl] = _rope64(qi[:, sl], c64, s64, first_half).astype(qi_ref.dtype)

    m = idx_dot(wm_ref)
    roped = _rope64(m, c64, s64, first_half)
    xf = m + bf_ref[...]
    logsig = jnp.minimum(xf, 0.0) - jnp.log1p(jnp.exp(-jnp.abs(xf)))
    wsc = m * (H_IDX ** -0.5)
    ki_only = jnp.where(lane < MISC_F, roped, 0.0)
    misc_ref[...] = jnp.where(lane < MISC_F, roped,
                              jnp.where(lane < MISC_W, logsig,
                                        jnp.where(lane < MISC_END, wsc, 0.0)))
    kia_ref[...] = ki_only.astype(BF16)
    kib_ref[...] = pltpu.roll(ki_only, D_IDX, axis=1).astype(BF16)


def _row_specs(tm, d, rows_per_group, mod_rows):
    tiles_per_group = rows_per_group // tm
    x_spec = pl.BlockSpec((tm, d), lambda i: (i, 0))
    mod_spec = pl.BlockSpec((None, mod_rows, d), lambda i: (i // tiles_per_group, 0, 0))
    return x_spec, mod_spec


def _full(shape):
    nd = len(shape)
    return pl.BlockSpec(shape, lambda i: (0,) * nd)


def _proj_fox(x2d, sc, sh, wq, wk, wv, gq, gk, tm, rows_per_group):
    m, d = x2d.shape
    x_spec, mod_spec = _row_specs(tm, d, rows_per_group, sc.shape[1])
    n = wq.shape[1]
    o_spec = pl.BlockSpec((tm, n), lambda i: (i, 0))
    return pl.pallas_call(
        _proj_fox_kernel,
        grid=(m // tm,),
        in_specs=[x_spec, mod_spec, mod_spec, _full(wq.shape), _full(wk.shape), _full(wv.shape),
                  _full(gq.shape), _full(gk.shape)],
        out_specs=[o_spec] * 5,
        out_shape=[jax.ShapeDtypeStruct((m, n), BF16)] * 3 + [jax.ShapeDtypeStruct((m, n), F32)] * 2,
        compiler_params=_cp(("parallel",)),
        name="proj_fox",
    )(x2d, sc, sh, wq, wk, wv, gq, gk)


def _proj_dsa(x2d, sc, sh, wq, wk, wv, wi, wm, gq, gk, bf, tabs, tm, rows_per_group, tiles_per_seq):
    m, d = x2d.shape
    x_spec, mod_spec = _row_specs(tm, d, rows_per_group, sc.shape[1])
    tab_spec = pl.BlockSpec((tm, LANES), lambda i: (i % tiles_per_seq, 0))
    nq, nk, ni = wq.shape[1], wk.shape[1], wi.shape[1]

    def ospec(n):
        return pl.BlockSpec((tm, n), lambda i: (i, 0))

    def oshape(n, dt):
        return jax.ShapeDtypeStruct((m, n), dt)

    return pl.pallas_call(
        _proj_dsa_kernel,
        grid=(m // tm,),
        in_specs=[x_spec, mod_spec, mod_spec, _full(wq.shape), _full(wk.shape), _full(wv.shape),
                  _full(wi.shape), _full(wm.shape), _full(gq.shape), _full(gk.shape), _full(bf.shape),
                  tab_spec, tab_spec, tab_spec, tab_spec],
        out_specs=[ospec(nq), ospec(nk), ospec(nk), ospec(ni), ospec(LANES), ospec(LANES),
                   ospec(nk), ospec(nk), ospec(LANES)],
        out_shape=[oshape(nq, BF16), oshape(nk, BF16), oshape(nk, BF16), oshape(ni, wi.dtype),
                   oshape(LANES, BF16), oshape(LANES, BF16),
                   oshape(nk, F32), oshape(nk, F32), oshape(LANES, F32)],
        compiler_params=_cp(("parallel",)),
        name="proj_dsa",
    )(x2d, sc, sh, wq, wk, wv, wi, wm, gq, gk, bf, *tabs)


def _cumsum_kernel(x_ref, o_ref):
    x = x_ref[...]
    n = x.shape[1]
    lane = lax.broadcasted_iota(I32, x.shape, 1)
    k = 1
    while k < n:
        x = x + jnp.where(lane >= k, pltpu.roll(x, k, axis=1), 0.0)
        k *= 2
    o_ref[...] = x


def _cumsum_lanes(x):
    return pl.pallas_call(
        _cumsum_kernel,
        out_shape=jax.ShapeDtypeStruct(x.shape, F32),
        name="logf_cumsum",
    )(x)


def _fox_flash_kernel(q_ref, k_ref, v_ref, ck_ref, o_ref, s_ref, *, t, n_q):
    qi = pl.program_id(2)
    q = (q_ref[...].astype(F32) * (HEAD_DIM ** -0.5)).astype(BF16)

    def lane_fold(x, op, init):
        for j in range(x.shape[1] // LANES):
            init = op(init, x[:, j * LANES:(j + 1) * LANES])
        return init

    def attend(n_tiles):
        mx = jnp.full((t, LANES), -jnp.inf, F32)
        for c in range(n_tiles):
            s = _dot_nt(q, k_ref[c * t:(c + 1) * t, :]) - ck_ref[c]
            if c == n_tiles - 1:
                s = jnp.where(lax.broadcasted_iota(I32, s.shape, 1) <= lax.broadcasted_iota(I32, s.shape, 0),
                              s, -jnp.inf)
            s_ref[:, c * t:(c + 1) * t] = s
            mx = lane_fold(s, jnp.maximum, mx)
        m = jnp.max(mx, axis=-1, keepdims=True)
        ls = jnp.zeros((t, LANES), F32)
        acc = jnp.zeros((t, HEAD_DIM), F32)
        for c in range(n_tiles):
            p = jnp.exp(s_ref[:, c * t:(c + 1) * t] - m)
            ls = lane_fold(p, jnp.add, ls)
            acc = acc + _dot(p.astype(BF16), v_ref[c * t:(c + 1) * t, :])
        o_ref[...] = (acc / jnp.sum(ls, axis=-1, keepdims=True)).astype(o_ref.dtype)

    for n_tiles in range(1, n_q + 1):
        pl.when(qi == n_tiles - 1)(functools.partial(attend, n_tiles))


def _fox_prompt(qb, kb, vb, cum, t=512):
    b, s, hd = qb.shape
    h = hd // HEAD_DIM
    ck = cum.reshape(b, h, s // t, 1, t)
    kv_spec = pl.BlockSpec((None, s, HEAD_DIM), lambda bi, hi, qi: (bi, 0, hi))
    q_spec = pl.BlockSpec((None, t, HEAD_DIM), lambda bi, hi, qi: (bi, qi, hi))
    return pl.pallas_call(
        functools.partial(_fox_flash_kernel, t=t, n_q=s // t),
        grid=(b, h, s // t),
        in_specs=[q_spec, kv_spec, kv_spec,
                  pl.BlockSpec((None, None, s // t, 1, t), lambda bi, hi, qi: (bi, hi, 0, 0, 0))],
        out_specs=q_spec,
        out_shape=jax.ShapeDtypeStruct((b, s, hd), BF16),
        scratch_shapes=[pltpu.VMEM((t, s), F32)],
        compiler_params=_cp(("parallel", "parallel", "arbitrary")),
        name="fox_prompt",
    )(qb, kb, vb, ck)


def _order_key(x):
    bits = pltpu.bitcast(x + 0.0, I32)
    return jnp.where(bits < 0, bits ^ jnp.int32(0x7FFFFFFF), bits)


def _kth_largest_key(key, k, reduce_axes):
    shape = list(key.shape)
    for ax in reduce_axes:
        shape[ax] = 1
    sign = jnp.int32(-2 ** 31)

    def body(i, t):
        bit = lax.shift_left(jnp.int32(1), jnp.int32(31) - i)
        cand = t | bit
        ge = jnp.where(key >= (cand ^ sign), 1.0, 0.0)
        cnt = jnp.sum(ge, axis=reduce_axes, keepdims=True)
        return jnp.where(cnt >= float(k), cand, t)

    t = lax.fori_loop(0, 32, body, jnp.zeros(shape, I32))
    return t ^ sign


def _kth_largest_value(x, k, reduce_axes):
    kf = float(k)

    def count_ge(v):
        return jnp.sum(jnp.where(x >= v, 1.0, 0.0), axis=reduce_axes, keepdims=True)

    present = x > -jnp.inf
    top = jnp.max(x, axis=reduce_axes, keepdims=True)
    lo0 = jnp.min(jnp.where(present, x, jnp.inf), axis=reduce_axes, keepdims=True)

    def body(i, carry):
        lo, hi = carry
        mid = lo + 0.5 * (hi - lo)
        ge = count_ge(mid) >= kf
        return jnp.where(ge, mid, lo), jnp.where(ge, hi, mid)

    _, hi = lax.fori_loop(0, BISECT_STEPS, body, (lo0, top))
    below = jnp.max(jnp.where(x < hi, x, -jnp.inf), axis=reduce_axes, keepdims=True)
    thr = jnp.where(count_ge(top) >= kf, top, below)
    n_present = jnp.sum(jnp.where(present, 1.0, 0.0), axis=reduce_axes, keepdims=True)
    return jnp.where(n_present < kf, -jnp.inf, thr)


def _dsa_prompt_kernel(qi_ref, kia_ref, kib_ref, misc_ref, qd_ref, kd_ref, vd_ref, o_ref,
                       score_ref, selb_ref, *, tq, n_sel, q0):
    s_len = kia_ref.shape[0]
    row = (pl.program_id(1) + q0) * tq + lax.broadcasted_iota(I32, (tq, s_len), 0)
    col = lax.broadcasted_iota(I32, (tq, s_len), 1)
    causal = col <= row

    misc = misc_ref[...]
    kia, kib = kia_ref[...], kib_ref[...]
    acc = jnp.zeros((tq, s_len), F32)
    for p in range(qi_ref.shape[1] // LANES):
        qblk = qi_ref[:, p * LANES:(p + 1) * LANES]
        for half, kmat in enumerate((kia, kib)):
            hh = 2 * p + half
            w = misc[:, MISC_W + hh:MISC_W + hh + 1] * (D_IDX ** -0.5)
            acc = acc + w * jnp.maximum(_dot_nt(qblk, kmat), 0.0)
    score_ref[...] = jnp.where(causal, acc, -jnp.inf)

    def count(mask):
        return jnp.sum(jnp.where(mask, 1.0, 0.0), axis=1, keepdims=True)

    def select_ranked(vals, thr):
        need = float(n_sel) - count(vals > thr)
        ch = 256
        r_i = lax.broadcasted_iota(I32, (ch, ch), 0)
        c_i = lax.broadcasted_iota(I32, (ch, ch), 1)
        tri = jnp.where(r_i < c_i, 1.0, 0.0).astype(BF16)
        base = jnp.zeros((tq, 1), F32)
        for c in range(s_len // ch):
            sl = slice(c * ch, (c + 1) * ch)
            eq_c = jnp.where(vals[:, sl] == thr, 1.0, 0.0)
            rank = base + _dot(eq_c.astype(BF16), tri)
            take = (vals[:, sl] > thr) | ((eq_c > 0.0) & (rank < need))
            selb_ref[:, sl] = jnp.where(take & causal[:, sl], 0.0, -jnp.inf)
            base = base + jnp.sum(eq_c, axis=1, keepdims=True)

    score = score_ref[...]
    thr = _kth_largest_value(score, n_sel, (1,))
    cnt_ge = count(score >= thr)
    selb_ref[...] = jnp.where((score >= thr) & causal, 0.0, -jnp.inf)
    finite_thr = thr > -jnp.inf
    tie_rows = jnp.where((cnt_ge > float(n_sel)) & finite_thr, 1.0, 0.0)
    unresolved = jnp.where((cnt_ge < float(n_sel)) & finite_thr, 1.0, 0.0)

    @pl.when(jnp.max(tie_rows) > 0.0)
    def _():
        select_ranked(score, thr)

    @pl.when(jnp.max(unresolved) > 0.0)
    def _():
        key = _order_key(score)
        select_ranked(key, _kth_largest_key(key, n_sel, (1,)))

    n_heads = qd_ref.shape[1] // HEAD_DIM
    rep = n_heads // KV_DSA
    for hh in range(n_heads):
        g = hh // rep
        q = qd_ref[:, hh * HEAD_DIM:(hh + 1) * HEAD_DIM]
        kg = kd_ref[:, g * HEAD_DIM:(g + 1) * HEAD_DIM]
        vg = vd_ref[:, g * HEAD_DIM:(g + 1) * HEAD_DIM]
        s = _dot_nt(q, kg) * (HEAD_DIM ** -0.5) + selb_ref[...]
        m = jnp.max(s, axis=-1, keepdims=True)
        p = jnp.exp(s - m)
        l = jnp.sum(p, axis=-1, keepdims=True)
        o_ref[:, hh * HEAD_DIM:(hh + 1) * HEAD_DIM] = (_dot(p.astype(BF16), vg) / l).astype(o_ref.dtype)


def _dsa_prompt(qi, kia, kib, misc, qd, kd, vd, tq=256, tiles_per_call=2):
    b, s, _ = qi.shape
    n_sel = min(TOPK_MAX, s // 4)
    outs = []
    for q0 in range(0, s // tq, tiles_per_call):
        s_eff = (q0 + tiles_per_call) * tq

        def qspec(n, q0=q0):
            return pl.BlockSpec((None, tq, n), lambda bi, i: (bi, i + q0, 0))

        def kspec(n, s_eff=s_eff):
            return pl.BlockSpec((None, s_eff, n), lambda bi, i: (bi, 0, 0))

        outs.append(pl.pallas_call(
            functools.partial(_dsa_prompt_kernel, tq=tq, n_sel=n_sel, q0=q0),
            grid=(b, tiles_per_call),
            in_specs=[qspec(qi.shape[2]), kspec(LANES), kspec(LANES), qspec(LANES),
                      qspec(qd.shape[2]), kspec(kd.shape[2]), kspec(vd.shape[2])],
            out_specs=pl.BlockSpec((None, tq, qd.shape[2]), lambda bi, i: (bi, i, 0)),
            out_shape=jax.ShapeDtypeStruct((b, tiles_per_call * tq, qd.shape[2]), BF16),
            scratch_shapes=[pltpu.VMEM((tq, s_eff), F32), pltpu.VMEM((tq, s_eff), F32)],
            compiler_params=_cp(("parallel", "parallel")),
            name="dsa_prompt",
        )(qi, kia, kib, misc, qd, kd, vd))
    return jnp.concatenate(outs, axis=1)


def _idx_score_rows(qi, w, kt_refs):
    rows = []
    for kt_ref in kt_refs:
        d = jnp.dot(qi, kt_ref[...], precision=lax.Precision.HIGHEST, preferred_element_type=F32)
        rows.append(jnp.sum(w * jnp.maximum(d, 0.0), axis=0, keepdims=True))
    return jnp.concatenate(rows, axis=0)


def _fox_sample_kernel(pt_ref, q_ref, kn_ref, vn_ref, lfn_ref, qi_ref, wi_ref, kin_ref, *refs, pp, n_heads):
    k_refs = refs[:pp]
    v_refs = refs[pp:2 * pp]
    lf_refs = refs[2 * pp:3 * pp]
    kt_refs = refs[3 * pp:4 * pp]
    o_ref, sc_ref, scn_ref, m_ref, l_ref, acc_ref, carry_ref = refs[4 * pp:]
    c = pl.program_id(1)
    width = PAGE_SIZE * n_heads
    q = q_ref[...] * (HEAD_DIM ** -0.5)
    qb = q.astype(BF16)
    qi = qi_ref[...]
    wi = wi_ref[...] * (D_IDX ** -0.5)
    own_head = (lax.broadcasted_iota(I32, (n_heads, width), 1) % n_heads
                == lax.broadcasted_iota(I32, (n_heads, width), 0))

    @pl.when(c == 0)
    def _():
        s_new = jnp.sum(q * kn_ref[...], axis=-1, keepdims=True)
        m_ref[...] = jnp.broadcast_to(s_new, m_ref.shape)
        l_ref[...] = jnp.ones(l_ref.shape, F32)
        acc_ref[...] = vn_ref[...]
        carry_ref[...] = lfn_ref[...]
        d = jnp.sum(qi * kin_ref[...], axis=-1, keepdims=True)
        sn = jnp.sum(wi * jnp.maximum(d, 0.0), axis=0, keepdims=True)
        r = lax.broadcasted_iota(I32, scn_ref.shape, 0)
        ln = lax.broadcasted_iota(I32, scn_ref.shape, 1)
        scn_ref[...] = jnp.where((r == 0) & (ln == 0), jnp.broadcast_to(sn, scn_ref.shape), -jnp.inf)

    sc_ref[...] = _idx_score_rows(qi, wi, kt_refs)

    lf = jnp.concatenate([r[...] for r in lf_refs], axis=0)
    lane = lax.broadcasted_iota(I32, lf.shape, 1)
    tot, suf = lf, lf
    k = n_heads
    while k < width:
        tot = tot + pltpu.roll(tot, k, axis=1)
        suf = suf + jnp.where(lane + k < width, pltpu.roll(suf, width - k, axis=1), 0.0)
        k *= 2
    later = suf - lf
    run = carry_ref[...]
    bias = [None] * pp
    for j in reversed(range(pp)):
        bias[j] = run + later[j:j + 1, :]
        run = run + tot[j:j + 1, :]
    carry_ref[...] = run

    logits = []
    for j in range(pp):
        s = _dot_nt(qb, k_refs[j][...].astype(BF16))
        logits.append(jnp.where(own_head, s + bias[j], NEG))
    m_prev = m_ref[...]
    m_new = m_prev
    for j in range(pp):
        m_new = jnp.maximum(m_new, jnp.max(logits[j], axis=-1, keepdims=True))
    alpha = jnp.exp(m_prev - m_new)
    l_new = alpha * l_ref[...]
    acc = alpha * acc_ref[...]
    for j in range(pp):
        p = jnp.exp(logits[j] - m_new[:, :1])
        l_new = l_new + jnp.sum(p, axis=-1, keepdims=True)
        acc = acc + _dot(p.astype(BF16), v_refs[j][...].astype(BF16))
    m_ref[...] = m_new
    l_ref[...] = l_new
    acc_ref[...] = acc

    @pl.when(c == pl.num_programs(1) - 1)
    def _():
        o_ref[...] = acc / l_new


def _fox_sample(page_table, q, k_new, v_new, lf_new, qi, wi, ki_new, cache_k, cache_v, cache_lf, cache_idx_t, pp=8):
    db, n_heads, _ = q.shape
    n_pages = page_table.shape[1]
    nc = n_pages // pp
    width = PAGE_SIZE * n_heads

    def page_idx(j):
        return lambda bi, ci, pt: (pt[bi, (nc - 1 - ci) * pp + j], 0, 0)

    def per_seq(shape):
        return pl.BlockSpec((None,) + shape, lambda bi, ci, pt: (bi, 0, 0))

    vec_spec = per_seq((n_heads, HEAD_DIM))
    kv_specs = [pl.BlockSpec((None, width, HEAD_DIM), page_idx(j)) for j in range(pp)]
    lf_specs = [pl.BlockSpec((None, 1, width), page_idx(j)) for j in range(pp)]
    kt_specs = [pl.BlockSpec((None, D_IDX, PAGE_SIZE), page_idx(j)) for j in range(pp)]
    grid_spec = pltpu.PrefetchScalarGridSpec(
        num_scalar_prefetch=1,
        grid=(db, nc),
        in_specs=[vec_spec, vec_spec, vec_spec, per_seq((1, width)),
                  per_seq((H_IDX, D_IDX)), per_seq((H_IDX, LANES)), per_seq((1, D_IDX))]
                 + kv_specs + kv_specs + lf_specs + kt_specs,
        out_specs=[vec_spec,
                   pl.BlockSpec((None, pp, PAGE_SIZE), lambda bi, ci, pt: (bi, nc - 1 - ci, 0)),
                   per_seq((8, LANES))],
        scratch_shapes=[pltpu.VMEM((n_heads, HEAD_DIM), F32)] * 3 + [pltpu.VMEM((1, width), F32)],
    )
    return pl.pallas_call(
        functools.partial(_fox_sample_kernel, pp=pp, n_heads=n_heads),
        grid_spec=grid_spec,
        out_shape=[jax.ShapeDtypeStruct((db, n_heads, HEAD_DIM), F32),
                   jax.ShapeDtypeStruct((db, n_pages, PAGE_SIZE), F32),
                   jax.ShapeDtypeStruct((db, 8, LANES), F32)],
        compiler_params=_cp(("parallel", "arbitrary")),
        name="fox_sample",
    )(page_table, q, k_new, v_new, lf_new, qi, wi, ki_new,
      *([cache_k] * pp), *([cache_v] * pp), *([cache_lf] * pp), *([cache_idx_t] * pp))


def _select_kernel(s_ref, past_ref, new_ref, take_ref, *, n_sel, n_pages):
    score = s_ref[...]
    db, rows, _ = score.shape
    valid = score > -jnp.inf

    def count(mask):
        return jnp.sum(jnp.where(mask, 1.0, 0.0), axis=(1, 2), keepdims=True)

    def select_ranked(vals, thr):
        need = float(n_sel) - count(vals > thr)
        eq = jnp.where(vals == thr, 1.0, 0.0)
        r_i = lax.broadcasted_iota(I32, (LANES, LANES), 0)
        c_i = lax.broadcasted_iota(I32, (LANES, LANES), 1)
        tri = jnp.where(r_i < c_i, 1.0, 0.0).astype(BF16)
        within = _dot(eq.reshape(db * rows, LANES).astype(BF16), tri).reshape(db, rows, LANES)
        row_cnt = jnp.broadcast_to(jnp.sum(eq, axis=2, keepdims=True), score.shape)
        rows_pad = 2 * LANES
        cnt_pad = jnp.concatenate([row_cnt, jnp.zeros((db, rows_pad - rows, LANES), F32)], axis=1)
        rr = lax.broadcasted_iota(I32, (db, rows_pad, rows_pad), 1)
        rc = lax.broadcasted_iota(I32, (db, rows_pad, rows_pad), 2)
        before = jnp.einsum("bij,bjl->bil", jnp.where(rc < rr, 1.0, 0.0).astype(BF16), cnt_pad.astype(BF16),
                            preferred_element_type=F32)[:, :rows, :]
        take = (vals > thr) | ((eq > 0.0) & (within + before < need))
        take_ref[...] = jnp.where(take & valid, 1.0, 0.0)

    thr = _kth_largest_value(score, n_sel, (1, 2))
    cnt_ge = count(score >= thr)
    take_ref[...] = jnp.where((score >= thr) & valid, 1.0, 0.0)
    finite_thr = thr > -jnp.inf
    ties = jnp.where((cnt_ge > float(n_sel)) & finite_thr, 1.0, 0.0)
    unresolved = jnp.where((cnt_ge < float(n_sel)) & finite_thr, 1.0, 0.0)

    @pl.when(jnp.max(ties) > 0.0)
    def _():
        select_ranked(score, thr)

    @pl.when(jnp.max(unresolved) > 0.0)
    def _():
        key = _order_key(score)
        select_ranked(key, _kth_largest_key(key, n_sel, (1, 2)))

    take01 = take_ref[...]
    new_ref[...] = jnp.where(take01[:, n_pages:, :] > 0.5, 0.0, NEG)
    t_i = lax.broadcasted_iota(I32, (LANES, LANES * KV_DSA), 0)
    l_i = lax.broadcasted_iota(I32, (LANES, LANES * KV_DSA), 1)
    spread = jnp.where(l_i // KV_DSA == t_i, 1.0, 0.0).astype(BF16)
    past = _dot(take01[:, :n_pages, :].reshape(db * n_pages, LANES).astype(BF16), spread)
    past_ref[...] = jnp.where(past > 0.5, 0.0, NEG).reshape(db, n_pages, LANES * KV_DSA)


def _select_sample(scores, n_sel, n_pages):
    db, rows, lanes = scores.shape
    return pl.pallas_call(
        functools.partial(_select_kernel, n_sel=n_sel, n_pages=n_pages),
        out_shape=[jax.ShapeDtypeStruct((db, n_pages, lanes * KV_DSA), F32),
                   jax.ShapeDtypeStruct((db, rows - n_pages, lanes), F32)],
        scratch_shapes=[pltpu.VMEM((db, rows, lanes), F32)],
        compiler_params=pltpu.CompilerParams(vmem_limit_bytes=VMEM_LIMIT),
        name="select_sample",
    )(scores)


def _dsa_sample_kernel(pt_ref, q_ref, kn_ref, vn_ref, selb_ref, selbn_ref, *refs, pp, n_heads):
    k_refs = refs[:pp]
    v_refs = refs[pp:2 * pp]
    o_ref, m_ref, l_ref, acc_ref = refs[2 * pp:]
    c = pl.program_id(1)
    rep = n_heads // KV_DSA
    width = PAGE_SIZE * KV_DSA
    q = q_ref[...] * (HEAD_DIM ** -0.5)
    qb = q.astype(BF16)
    grp = lax.broadcasted_iota(I32, (n_heads, HEAD_DIM), 0) // rep
    own_kv = (lax.broadcasted_iota(I32, (n_heads, width), 1) % KV_DSA
              == lax.broadcasted_iota(I32, (n_heads, width), 0) // rep)

    @pl.when(c == 0)
    def _():
        kn = kn_ref[...]
        vn = vn_ref[...]
        s_new = jnp.zeros((n_heads, 1), F32)
        v_rows = jnp.zeros((n_heads, HEAD_DIM), F32)
        for g in range(KV_DSA):
            sg = jnp.sum(q * kn[g:g + 1, :], axis=-1, keepdims=True)
            s_new = jnp.where(grp[:, :1] == g, sg, s_new)
            v_rows = jnp.where(grp == g, jnp.broadcast_to(vn[g:g + 1, :], v_rows.shape), v_rows)
        s_new = s_new + selbn_ref[0:1, 0:1]
        m_ref[...] = jnp.broadcast_to(jnp.maximum(s_new, NEG), m_ref.shape)
        p_new = jnp.exp(s_new - m_ref[:, :1])
        l_ref[...] = jnp.broadcast_to(p_new, l_ref.shape)
        acc_ref[...] = p_new * v_rows

    selb = selb_ref[...]
    logits = []
    for j in range(pp):
        s = _dot_nt(qb, k_refs[j][...].astype(BF16))
        logits.append(jnp.where(own_kv, s + selb[j:j + 1, :], NEG))

    m_prev = m_ref[...]
    m_new = m_prev
    for j in range(pp):
        m_new = jnp.maximum(m_new, jnp.max(logits[j], axis=-1, keepdims=True))
    alpha = jnp.exp(m_prev - m_new)
    l_new = alpha * l_ref[...]
    acc = alpha * acc_ref[...]
    for j in range(pp):
        p = jnp.exp(logits[j] - m_new[:, :1])
        l_new = l_new + jnp.sum(p, axis=-1, keepdims=True)
        acc = acc + _dot(p.astype(BF16), v_refs[j][...].astype(BF16))
    m_ref[...] = m_new
    l_ref[...] = l_new
    acc_ref[...] = acc

    @pl.when(c == pl.num_programs(1) - 1)
    def _():
        o_ref[...] = acc / l_new


def _dsa_sample(page_table, q, k_new, v_new, selb_past, selb_new, cache_k, cache_v, pp=16):
    db, n_heads, _ = q.shape
    n_pages = page_table.shape[1]
    nc = n_pages // pp
    width = PAGE_SIZE * KV_DSA

    def page_idx(j):
        return lambda bi, ci, pt: (pt[bi, ci * pp + j], 0, 0)

    q_spec = pl.BlockSpec((None, n_heads, HEAD_DIM), lambda bi, ci, pt: (bi, 0, 0))
    n_spec = pl.BlockSpec((None, KV_DSA, HEAD_DIM), lambda bi, ci, pt: (bi, 0, 0))
    kv_specs = [pl.BlockSpec((None, width, HEAD_DIM), page_idx(j)) for j in range(pp)]
    grid_spec = pltpu.PrefetchScalarGridSpec(
        num_scalar_prefetch=1,
        grid=(db, nc),
        in_specs=[q_spec, n_spec, n_spec,
                  pl.BlockSpec((None, pp, width), lambda bi, ci, pt: (bi, ci, 0)),
                  pl.BlockSpec((None,) + selb_new.shape[1:], lambda bi, ci, pt: (bi, 0, 0))]
                 + kv_specs + kv_specs,
        out_specs=q_spec,
        scratch_shapes=[pltpu.VMEM((n_heads, HEAD_DIM), F32)] * 3,
    )
    return pl.pallas_call(
        functools.partial(_dsa_sample_kernel, pp=pp, n_heads=n_heads),
        grid_spec=grid_spec,
        out_shape=jax.ShapeDtypeStruct((db, n_heads, HEAD_DIM), F32),
        compiler_params=_cp(("parallel", "arbitrary")),
        name="dsa_sample",
    )(page_table, q, k_new, v_new, selb_past, selb_new, *([cache_k] * pp), *([cache_v] * pp))


def _outproj_kernel(x_ref, of_ref, od_ref, wf_ref, wd_ref, ga_ref, sc_ref, sh_ref, wr_ref,
                    x1_ref, h2_ref, route_ref):
    def proj(a_ref, w_ref):
        if w_ref.dtype == F32:
            return jnp.dot(a_ref[...].astype(F32), w_ref[...], precision=lax.Precision.HIGHEST,
                           preferred_element_type=F32)
        return _dot(a_ref[...].astype(BF16), w_ref[...])

    mix = proj(of_ref, wf_ref) + proj(od_ref, wd_ref)
    x1 = x_ref[...] + ga_ref[...] * mix
    x1_ref[...] = x1
    h2 = _modulated(x1, sc_ref, sh_ref)
    h2_ref[...] = h2
    if wr_ref.dtype == F32:
        logits = jnp.dot(h2, wr_ref[...], precision=lax.Precision.HIGHEST, preferred_element_type=F32)
    else:
        logits = _dot(h2.astype(BF16), wr_ref[...])
    lane = lax.broadcasted_iota(I32, logits.shape, 1).astype(F32)

    def first_argmax(vals, mask):
        v = jnp.where(mask, vals, -jnp.inf)
        mx = jnp.max(v, axis=-1, keepdims=True)
        idx = jnp.min(jnp.where(v == mx, lane, float(LANES)), axis=-1, keepdims=True)
        return mx, idx

    is_g = lane < float(N_GROUPS)
    mg, g_sel = first_argmax(logits, is_g)
    p_gsel = 1.0 / jnp.sum(jnp.where(is_g, jnp.exp(logits - mg), 0.0), axis=-1, keepdims=True)
    lo = float(N_GROUPS) + g_sel * float(EXPERTS_PER_GROUP)
    in_grp = (lane >= lo) & (lane < lo + float(EXPERTS_PER_GROUP))
    m1, i1 = first_argmax(logits, in_grp)
    m2, i2 = first_argmax(logits, in_grp & (lane != i1))
    e2 = jnp.exp(m2 - m1)
    w1 = p_gsel / (1.0 + e2)
    w2 = p_gsel * e2 / (1.0 + e2)
    route_ref[...] = jnp.where(lane == 0.0, i1 - float(N_GROUPS),
                               jnp.where(lane == 1.0, i2 - float(N_GROUPS),
                                         jnp.where(lane == 2.0, w1, jnp.where(lane == 3.0, w2, 0.0))))


def _outproj(x2d, o_fox, o_dsa, wf, wd, ga, sc, sh, wr, tm, rows_per_group):
    m, d = x2d.shape
    x_spec, mod_spec = _row_specs(tm, d, rows_per_group, sc.shape[1])
    half = o_fox.shape[1]
    half_spec = pl.BlockSpec((tm, half), lambda i: (i, 0))
    return pl.pallas_call(
        _outproj_kernel,
        grid=(m // tm,),
        in_specs=[x_spec, half_spec, half_spec, _full(wf.shape), _full(wd.shape),
                  mod_spec, mod_spec, mod_spec, _full(wr.shape)],
        out_specs=[x_spec, x_spec, pl.BlockSpec((tm, LANES), lambda i: (i, 0))],
        out_shape=[jax.ShapeDtypeStruct((m, d), F32), jax.ShapeDtypeStruct((m, d), F32),
                   jax.ShapeDtypeStruct((m, LANES), F32)],
        compiler_params=_cp(("parallel",)),
        name="outproj_router",
    )(x2d, o_fox, o_dsa, wf, wd, ga, sc, sh, wr)


def _dispatch_kernel(ends_ref, s0_ref, s1_ref, h_ref, t0_ref, t1_ref, ht_ref, xs_hbm, zbuf, sem, zsem,
                     *, tm, n_main):
    i = pl.program_id(0)

    @pl.when(i == 0)
    def _():
        zbuf[...] = jnp.zeros(zbuf.shape, F32)

        def fill(e):
            start = pl.multiple_of(jnp.maximum(ends_ref[0, e] - tm, 0), tm)
            return pltpu.make_async_copy(zbuf, xs_hbm.at[pl.ds(start, tm)], zsem)

        for e in range(N_EXPERTS):
            fill(e).start()
        for e in range(N_EXPERTS):
            fill(e).wait()

        def clear_unused(k, carry):
            cp = pltpu.make_async_copy(zbuf, xs_hbm.at[pl.ds(pl.multiple_of(k * tm, tm), tm)], zsem)
            cp.start()
            cp.wait()
            return carry

        lax.fori_loop(ends_ref[0, N_EXPERTS - 1] // tm, xs_hbm.shape[0] // tm, clear_unused, 0)

    def scatter(a_ref, b_ref, src_ref, rows):
        def issue(r, carry):
            src = src_ref.at[pl.ds(r, 1)]
            pltpu.make_async_copy(src, xs_hbm.at[pl.ds(a_ref[0, r], 1)], sem).start()
            pltpu.make_async_copy(src, xs_hbm.at[pl.ds(b_ref[0, r], 1)], sem).start()
            return carry

        lax.fori_loop(0, rows, issue, 0, unroll=8)
        pltpu.make_async_copy(src_ref, xs_hbm.at[pl.ds(0, rows)], sem).wait()
        pltpu.make_async_copy(src_ref, xs_hbm.at[pl.ds(0, rows)], sem).wait()

    @pl.when(i < n_main)
    def _():
        scatter(s0_ref, s1_ref, h_ref, tm)

    @pl.when(i == n_main)
    def _():
        scatter(t0_ref, t1_ref, ht_ref, ht_ref.shape[0])


def _dispatch(ends, slots_main, h_main, slots_tail, h_tail, n_slots, tm):
    m, d = h_main.shape
    mt = h_tail.shape[0]
    n_main = m // tm
    smem = pltpu.SMEM

    def main_idx(i):
        return (jnp.minimum(i, n_main - 1), 0, 0)

    id_spec = pl.BlockSpec((None, 1, tm), main_idx, memory_space=smem)
    tail_spec = pl.BlockSpec((None, 1, mt), lambda i: (0, 0, 0), memory_space=smem)
    return pl.pallas_call(
        functools.partial(_dispatch_kernel, tm=tm, n_main=n_main),
        grid=(n_main + 1,),
        in_specs=[pl.BlockSpec(memory_space=smem), id_spec, id_spec,
                  pl.BlockSpec((tm, d), lambda i: (jnp.minimum(i, n_main - 1), 0)),
                  tail_spec, tail_spec, pl.BlockSpec((mt, d), lambda i: (0, 0))],
        out_specs=pl.BlockSpec(memory_space=pl.ANY),
        out_shape=jax.ShapeDtypeStruct((n_slots, d), F32),
        scratch_shapes=[pltpu.VMEM((tm, d), F32), pltpu.SemaphoreType.DMA(()), pltpu.SemaphoreType.DMA(())],
        compiler_params=_cp(("arbitrary",)),
        name="moe_dispatch",
    )(ends.reshape(1, N_EXPERTS),
      slots_main[:, 0].reshape(n_main, 1, tm), slots_main[:, 1].reshape(n_main, 1, tm), h_main,
      slots_tail[:, 0].reshape(1, 1, mt), slots_tail[:, 1].reshape(1, 1, mt), h_tail)


def _moe_kernel(te_ref, na_ref, x_ref, wg_ref, wu_ref, wd_ref, o_ref, wg_b, wu_b, wd_b):
    t = pl.program_id(0)

    @pl.when(t < na_ref[0])
    def _():
        @pl.when((t == 0) | (te_ref[t] != te_ref[jnp.maximum(t - 1, 0)]))
        def _():
            wg_b[...] = wg_ref[...].astype(BF16)
            wu_b[...] = wu_ref[...].astype(BF16)
            wd_b[...] = wd_ref[...].astype(BF16)

        h = x_ref[...].astype(BF16)
        a = _dot(h, wg_b[...])
        u = _dot(h, wu_b[...])
        act = (a / (1.0 + jnp.exp(-a))) * u
        o_ref[...] = _dot(act.astype(BF16), wd_b[...])

    @pl.when(t >= na_ref[0])
    def _():
        o_ref[...] = jnp.zeros(o_ref.shape, F32)


def _moe_experts(tile_expert, n_active, x_sorted, w_gate, w_up, w_down, tm):
    n_tiles = tile_expert.shape[0]
    d = x_sorted.shape[1]
    f = w_gate.shape[2]

    def row_idx(t, te, na):
        return (jnp.minimum(t, na[0] - 1), 0)

    def w_idx(t, te, na):
        return (te[jnp.minimum(t, na[0] - 1)], 0, 0)

    grid_spec = pltpu.PrefetchScalarGridSpec(
        num_scalar_prefetch=2,
        grid=(n_tiles,),
        in_specs=[pl.BlockSpec((tm, d), row_idx),
                  pl.BlockSpec((None, d, f), w_idx),
                  pl.BlockSpec((None, d, f), w_idx),
                  pl.BlockSpec((None, f, d), w_idx)],
        out_specs=pl.BlockSpec((tm, d), lambda t, te, na: (t, 0)),
        scratch_shapes=[pltpu.VMEM((d, f), BF16), pltpu.VMEM((d, f), BF16), pltpu.VMEM((f, d), BF16)],
    )
    return pl.pallas_call(
        _moe_kernel,
        grid_spec=grid_spec,
        out_shape=jax.ShapeDtypeStruct((n_tiles * tm, d), F32),
        compiler_params=_cp(("arbitrary",)),
        name="moe_experts",
    )(tile_expert, n_active, x_sorted, w_gate, w_up, w_down)


def _combine_kernel(s0_ref, s1_ref, x1_ref, gm_ref, route_ref, y_hbm, o_ref, b0, b1, sem, *, tm):
    def copies(r):
        return (pltpu.make_async_copy(y_hbm.at[pl.ds(s0_ref[0, r], 1)], b0.at[pl.ds(r, 1)], sem),
                pltpu.make_async_copy(y_hbm.at[pl.ds(s1_ref[0, r], 1)], b1.at[pl.ds(r, 1)], sem))

    def issue(r, carry):
        c0, c1 = copies(r)
        c0.start()
        c1.start()
        return carry

    lax.fori_loop(0, tm, issue, 0, unroll=8)
    pltpu.make_async_copy(y_hbm.at[pl.ds(0, tm)], b0, sem).wait()
    pltpu.make_async_copy(y_hbm.at[pl.ds(0, tm)], b1, sem).wait()
    route = route_ref[...]
    moe = route[:, 2:3] * b0[...] + route[:, 3:4] * b1[...]
    o_ref[...] = x1_ref[...] + gm_ref[...] * moe


def _combine(slot0, slot1, x1, gm, route, y_slots, tm, rows_per_group):
    m, d = x1.shape
    x_spec, mod_spec = _row_specs(tm, d, rows_per_group, gm.shape[1])
    n_tiles = m // tm
    id_spec = pl.BlockSpec((None, 1, tm), lambda i: (i, 0, 0), memory_space=pltpu.SMEM)
    return pl.pallas_call(
        functools.partial(_combine_kernel, tm=tm),
        grid=(n_tiles,),
        in_specs=[id_spec, id_spec, x_spec, mod_spec,
                  pl.BlockSpec((tm, LANES), lambda i: (i, 0)),
                  pl.BlockSpec(memory_space=pl.ANY)],
        out_specs=x_spec,
        out_shape=jax.ShapeDtypeStruct((m, d), F32),
        scratch_shapes=[pltpu.VMEM((tm, d), F32), pltpu.VMEM((tm, d), F32), pltpu.SemaphoreType.DMA(())],
        compiler_params=_cp(("arbitrary",)),
        name="moe_combine",
    )(slot0.reshape(n_tiles, 1, tm), slot1.reshape(n_tiles, 1, tm), x1, gm, route, y_slots)


def _moe_plan(expert_ids, tm):
    t_all = expert_ids.shape[0]
    flat = expert_ids.reshape(-1)
    onehot = (flat[:, None] == jnp.arange(N_EXPERTS, dtype=I32)[None, :]).astype(I32)
    rank = jnp.sum((jnp.cumsum(onehot, axis=0) - onehot) * onehot, axis=1)
    counts = jnp.sum(onehot, axis=0)
    padded = ((counts + tm - 1) // tm) * tm
    ends = jnp.cumsum(padded).astype(I32)
    starts = ends - padded
    slot = jnp.sum(onehot * starts[None, :], axis=1) + rank
    n_tiles = (2 * t_all + N_EXPERTS * (tm - 1)) // tm + 1
    tile_start = jnp.arange(n_tiles, dtype=I32) * tm
    tile_expert = jnp.minimum(jnp.sum((tile_start[:, None] >= ends[None, :]).astype(I32), axis=1),
                              N_EXPERTS - 1)
    n_active = (ends[-1:] // tm).astype(I32)
    return tile_expert.astype(I32), n_active, ends, slot.reshape(t_all, 2).astype(I32)


def _rope_tables(pos, dim, lanes):
    half = dim // 2
    inv = ROPE_THETA ** (-jnp.arange(half, dtype=F32) / half)
    ang = pos.astype(F32)[:, None] * inv[None, :]
    cos, sin = jnp.cos(ang), jnp.sin(ang)
    reps = lanes // dim
    cos_t = jnp.tile(jnp.concatenate([cos, cos], axis=-1), (1, reps))
    sin_t = jnp.tile(jnp.concatenate([-sin, sin], axis=-1), (1, reps))
    return cos_t, sin_t


def kernel(x_prompt, x_sample, cache_fox_k, cache_fox_v, cache_fox_logf, cache_dsa_k, cache_dsa_v,
           cache_idx_k, page_table, c_prompt, c_sample, w_in, b_forget, q_gain_fox, k_gain_fox,
           q_gain_dsa, k_gain_dsa, w_out, w_ada, b_ada, w_router_group, w_router_expert,
           w_gate, w_up, w_down):
    b, s, d = x_prompt.shape
    db, dq, _ = x_sample.shape
    depth = w_in.shape[0]
    assert depth == 1 and dq == 1
    past_len = page_table.shape[1] * PAGE_SIZE
    n_phys = cache_fox_k.shape[1]
    h_fox = cache_fox_k.shape[3]
    h_dsa = d // (2 * HEAD_DIM)
    t_p, t_s = b * s, db * dq
    t_all = t_p + t_s
    tm_p, tm_s = 256, t_s
    layer = 0

    n_c = b + db
    pad = (-n_c) % 8
    c_all = jnp.concatenate([c_prompt, c_sample, jnp.zeros((pad, d), F32)], axis=0)
    mod = _adaln(c_all, w_ada[layer], b_ada[layer])
    mods = [mod[:, i * d:(i + 1) * d] for i in range(6)]
    mod_p = [m_[:b].reshape(b, 1, d) for m_ in mods]
    mod_s = [m_[b:b + db].reshape(1, db, d) for m_ in mods]

    w = w_in[layer]
    nf = h_fox * HEAD_DIM
    nd = h_dsa * HEAD_DIM
    nkv = KV_DSA * HEAD_DIM
    ni = H_IDX * D_IDX
    o = 0
    wqf = w[:, o:o + nf].astype(BF16); o += nf
    wkf = w[:, o:o + nf].astype(BF16); o += nf
    wvf = w[:, o:o + nf].astype(BF16); o += nf
    w_fl = w[:, o:o + h_fox]; o += h_fox
    wqd = w[:, o:o + nd].astype(BF16); o += nd
    wkd = w[:, o:o + nkv].astype(BF16); o += nkv
    wvd = w[:, o:o + nkv].astype(BF16); o += nkv
    wqi = w[:, o:o + ni]; o += ni
    w_ki = w[:, o:o + D_IDX]; o += D_IDX
    w_wi = w[:, o:o + H_IDX]; o += H_IDX
    assert h_fox <= MISC_W - MISC_F
    wmisc = jnp.concatenate([w_ki, w_fl, jnp.zeros((d, MISC_W - MISC_F - h_fox), F32), w_wi,
                             jnp.zeros((d, LANES - MISC_END), F32)], axis=1)
    bf = jnp.zeros((1, LANES), F32).at[0, MISC_F:MISC_F + h_fox].set(b_forget[layer])
    gqf, gkf = q_gain_fox[layer].reshape(1, HEAD_DIM), k_gain_fox[layer].reshape(1, HEAD_DIM)
    gqd, gkd = q_gain_dsa[layer].reshape(1, HEAD_DIM), k_gain_dsa[layer].reshape(1, HEAD_DIM)

    pos_p = jnp.arange(s, dtype=I32)
    pos_s = jnp.full((t_s,), past_len, I32)
    tabs_p = _rope_tables(pos_p, HEAD_DIM, LANES) + _rope_tables(pos_p, D_IDX, LANES)
    tabs_s = _rope_tables(pos_s, HEAD_DIM, LANES) + _rope_tables(pos_s, D_IDX, LANES)

    xp2 = x_prompt.reshape(t_p, d)
    xs2 = x_sample.reshape(t_s, d)

    def project(x2d, mod_, tabs, tm, rows_per_group, tiles_per_seq, idx_dtype):
        fox = _proj_fox(x2d, mod_[1], mod_[0], wqf, wkf, wvf, gqf, gkf, tm, rows_per_group)
        dsa = _proj_dsa(x2d, mod_[1], mod_[0], wqd, wkd, wvd, wqi.astype(idx_dtype), wmisc.astype(idx_dtype),
                        gqd, gkd, bf, tabs, tm, rows_per_group, tiles_per_seq)
        return fox, dsa

    (qf_p, kfb_p, vfb_p, kf_p, vf_p), (qd_p, kdb_p, vdb_p, qi_p, kia_p, kib_p, kd_p, vd_p, misc_p) = \
        project(xp2, mod_p, tabs_p, tm_p, s, s // tm_p, BF16)
    (qf_s, _, _, kf_s, vf_s), (qd_s, _, _, qi_s, _, _, kd_s, vd_s, misc_s) = \
        project(xs2, mod_s, tabs_s, tm_s, t_s, 1, F32)

    lf_p = misc_p[:, MISC_F:MISC_F + h_fox].reshape(b, s, h_fox)
    cum = _cumsum_lanes(lf_p.transpose(0, 2, 1).reshape(b * h_fox, s)).reshape(b, h_fox, s)
    o_fox_p = _fox_prompt(qf_p.reshape(b, s, nf), kfb_p.reshape(b, s, nf), vfb_p.reshape(b, s, nf), cum)
    o_dsa_p = _dsa_prompt(qi_p.reshape(b, s, ni), kia_p.reshape(b, s, LANES), kib_p.reshape(b, s, LANES),
                          misc_p.reshape(b, s, LANES), qd_p.reshape(b, s, nd),
                          kdb_p.reshape(b, s, nkv), vdb_p.reshape(b, s, nkv))

    lf_s = misc_s[:, MISC_F:MISC_F + h_fox]
    ck = cache_fox_k[layer].reshape(n_phys, PAGE_SIZE * h_fox, HEAD_DIM)
    cv = cache_fox_v[layer].reshape(n_phys, PAGE_SIZE * h_fox, HEAD_DIM)
    clf = cache_fox_logf[layer].astype(F32).reshape(n_phys, 1, PAGE_SIZE * h_fox)
    w_s = misc_s[:, MISC_W:MISC_W + H_IDX]
    n_pages = page_table.shape[1]
    o_fox_s, sc_past, sc_new = _fox_sample(
        page_table,
        qf_s.astype(F32).reshape(db, h_fox, HEAD_DIM),
        kf_s.reshape(db, h_fox, HEAD_DIM), vf_s.reshape(db, h_fox, HEAD_DIM),
        jnp.tile(lf_s, (1, PAGE_SIZE)).reshape(db, 1, PAGE_SIZE * h_fox),
        qi_s.astype(F32).reshape(db, H_IDX, D_IDX),
        jnp.broadcast_to(w_s[:, :, None], (db, H_IDX, LANES)),
        misc_s[:, MISC_KI:MISC_KI + D_IDX].reshape(db, 1, D_IDX),
        ck, cv, clf, cache_idx_k[layer].transpose(0, 2, 1))
    n_sel = min(TOPK_MAX, (past_len + dq) // 4)
    selb_past, selb_new = _select_sample(jnp.concatenate([sc_past, sc_new], axis=1), n_sel, n_pages)
    cdk = cache_dsa_k[layer].reshape(n_phys, PAGE_SIZE * KV_DSA, HEAD_DIM)
    cdv = cache_dsa_v[layer].reshape(n_phys, PAGE_SIZE * KV_DSA, HEAD_DIM)
    o_dsa_s = _dsa_sample(page_table, qd_s.astype(F32).reshape(db, h_dsa, HEAD_DIM),
                          kd_s.reshape(db, KV_DSA, HEAD_DIM), vd_s.reshape(db, KV_DSA, HEAD_DIM),
                          selb_past, selb_new, cdk, cdv)

    wo = w_out[layer].astype(BF16)
    wof, wod = wo[:nf], wo[nf:]
    wr = jnp.concatenate([w_router_group[layer],
                          w_router_expert[layer].transpose(1, 0, 2).reshape(d, N_EXPERTS),
                          jnp.zeros((d, LANES - N_GROUPS - N_EXPERTS), F32)], axis=1)
    x1_p, h2_p, route_p = _outproj(xp2, o_fox_p.reshape(t_p, nf), o_dsa_p.reshape(t_p, nd), wof, wod,
                                   mod_p[2], mod_p[4], mod_p[3], wr.astype(BF16), tm_p, s)
    x1_s, h2_s, route_s = _outproj(xs2, o_fox_s.reshape(t_s, nf), o_dsa_s.reshape(t_s, nd),
                                   w_out[layer][:nf], w_out[layer][nf:],
                                   mod_s[2], mod_s[4], mod_s[3], wr, tm_s, t_s)

    expert_ids = jnp.concatenate([route_p[:, :2], route_s[:, :2]], axis=0).astype(I32)
    tile_expert, n_active, ends, slots = _moe_plan(expert_ids, tm_p)
    n_slots = tile_expert.shape[0] * tm_p
    x_sorted = _dispatch(ends, slots[:t_p], h2_p, slots[t_p:], h2_s, n_slots, tm_p)
    y_slots = _moe_experts(tile_expert, n_active, x_sorted, w_gate[layer], w_up[layer], w_down[layer], tm_p)
    y_p = _combine(slots[:t_p, 0], slots[:t_p, 1], x1_p, mod_p[5], route_p, y_slots, tm_p, s)
    y_s = _combine(slots[t_p:, 0], slots[t_p:, 1], x1_s, mod_s[5], route_s, y_slots, tm_s, t_s)

    def rows(kf, vf, misc, kd, vd, g, t):
        return (kf.reshape(1, g, t, h_fox, HEAD_DIM), vf.reshape(1, g, t, h_fox, HEAD_DIM),
                misc[:, MISC_F:MISC_F + h_fox].reshape(1, g, t, h_fox),
                kd.reshape(1, g, t, KV_DSA, HEAD_DIM), vd.reshape(1, g, t, KV_DSA, HEAD_DIM),
                misc[:, MISC_KI:MISC_KI + D_IDX].reshape(1, g, t, D_IDX))

    return ((y_p.reshape(b, s, d), y_s.reshape(db, dq, d))
            + rows(kf_p, vf_p, misc_p, kd_p, vd_p, b, s)
            + rows(kf_s, vf_s, misc_s, kd_s, vd_s, db, dq))
```

```python
import functools

import jax
import jax.numpy as jnp
from jax import lax
from jax.experimental import pallas as pl
from jax.experimental.pallas import tpu as pltpu

F32 = jnp.float32
BF16 = jnp.bfloat16
I32 = jnp.int32

HEAD_DIM = 128
D_IDX = 64
H_IDX = 16
KV_DSA = 2
TOPK_MAX = 256
N_GROUPS = 4
EXPERTS_PER_GROUP = 8
N_EXPERTS = N_GROUPS * EXPERTS_PER_GROUP
ROPE_THETA = 10000.0
EPS = 1e-6
PAGE_SIZE = 128

LANES = 128
NEG = -1e30
BISECT_STEPS = 36
VMEM_LIMIT = 52 * 1024 * 1024

MISC_KI = 0
MISC_F = 64
MISC_W = 72
MISC_END = 88


def _cp(sem):
    return pltpu.CompilerParams(dimension_semantics=sem, vmem_limit_bytes=VMEM_LIMIT)


def _dot(a, b):
    return jnp.dot(a, b, preferred_element_type=F32)


def _dot_nt(a, b):
    return lax.dot_general(a, b, (((1,), (1,)), ((), ())), preferred_element_type=F32)


def _adaln_kernel(c_ref, w_ref, b_ref, o_ref):
    c = c_ref[...]
    s = c / (1.0 + jnp.exp(-c))
    o_ref[...] = _dot(s.astype(BF16), w_ref[...].astype(BF16)) + b_ref[...]


def _adaln(c_all, w_ada, b_ada):
    rows, d = c_all.shape
    n = w_ada.shape[1]
    tn = 1024
    return pl.pallas_call(
        _adaln_kernel,
        grid=(n // tn,),
        in_specs=[pl.BlockSpec((rows, d), lambda j: (0, 0)),
                  pl.BlockSpec((d, tn), lambda j: (0, j)),
                  pl.BlockSpec((1, tn), lambda j: (0, j))],
        out_specs=pl.BlockSpec((rows, tn), lambda j: (0, j)),
        out_shape=jax.ShapeDtypeStruct((rows, n), F32),
        compiler_params=_cp(("parallel",)),
        name="adaln",
    )(c_all, w_ada, b_ada.reshape(1, n))


def _modulated(x, sc_ref, sh_ref):
    ms = jnp.mean(x * x, axis=-1, keepdims=True)
    return (x * lax.rsqrt(ms + EPS)) * (1.0 + sc_ref[...]) + sh_ref[...]


def _rms_heads(acc, gain, n_heads):
    outs = []
    for hh in range(n_heads):
        blk = acc[:, hh * HEAD_DIM:(hh + 1) * HEAD_DIM]
        ms = jnp.mean(blk * blk, axis=-1, keepdims=True)
        outs.append(blk * lax.rsqrt(ms + EPS) * gain)
    return outs


def _rope128(y, cos, sin_signed):
    return y * cos + pltpu.roll(y, HEAD_DIM // 2, axis=1) * sin_signed


def _rope64(y, cos, sin_signed, first_half):
    swapped = jnp.where(first_half, pltpu.roll(y, LANES - D_IDX // 2, axis=1),
                        pltpu.roll(y, D_IDX // 2, axis=1))
    return y * cos + swapped * sin_signed


def _proj_fox_kernel(x_ref, sc_ref, sh_ref, wq_ref, wk_ref, wv_ref, gq_ref, gk_ref,
                     qb_ref, kb_ref, vb_ref, k_ref, v_ref):
    h = _modulated(x_ref[...], sc_ref, sh_ref).astype(BF16)
    q = _rms_heads(_dot(h, wq_ref[...]), gq_ref[...], wq_ref.shape[1] // HEAD_DIM)
    for hh, blk in enumerate(q):
        qb_ref[:, hh * HEAD_DIM:(hh + 1) * HEAD_DIM] = blk.astype(BF16)
    k = _rms_heads(_dot(h, wk_ref[...]), gk_ref[...], wk_ref.shape[1] // HEAD_DIM)
    for hh, blk in enumerate(k):
        sl = slice(hh * HEAD_DIM, (hh + 1) * HEAD_DIM)
        k_ref[:, sl] = blk
        kb_ref[:, sl] = blk.astype(BF16)
    v = _dot(h, wv_ref[...])
    v_ref[...] = v
    vb_ref[...] = v.astype(BF16)


def _proj_dsa_kernel(x_ref, sc_ref, sh_ref, wq_ref, wk_ref, wv_ref, wi_ref, wm_ref,
                     gq_ref, gk_ref, bf_ref, c128_ref, s128_ref, c64_ref, s64_ref,
                     qb_ref, kb_ref, vb_ref, qi_ref, kia_ref, kib_ref, k_ref, v_ref, misc_ref):
    hf = _modulated(x_ref[...], sc_ref, sh_ref)
    h = hf.astype(BF16)

    def idx_dot(w_ref):
        if w_ref.dtype == F32:
            return jnp.dot(hf, w_ref[...], precision=lax.Precision.HIGHEST, preferred_element_type=F32)
        return _dot(h, w_ref[...])

    c128, s128 = c128_ref[...], s128_ref[...]
    c64, s64 = c64_ref[...], s64_ref[...]
    lane = lax.broadcasted_iota(I32, c64.shape, 1)
    first_half = (lane % D_IDX) < (D_IDX // 2)

    q = _rms_heads(_dot(h, wq_ref[...]), gq_ref[...], wq_ref.shape[1] // HEAD_DIM)
    for hh, blk in enumerate(q):
        qb_ref[:, hh * HEAD_DIM:(hh + 1) * HEAD_DIM] = _rope128(blk, c128, s128).astype(BF16)
    k = _rms_heads(_dot(h, wk_ref[...]), gk_ref[...], wk_ref.shape[1] // HEAD_DIM)
    for hh, blk in enumerate(k):
        sl = slice(hh * HEAD_DIM, (hh + 1) * HEAD_DIM)
        r = _rope128(blk, c128, s128)
        k_ref[:, sl] = r
        kb_ref[:, sl] = r.astype(BF16)
    v = _dot(h, wv_ref[...])
    v_ref[...] = v
    vb_ref[...] = v.astype(BF16)

    qi = idx_dot(wi_ref)
    for p in range(wi_ref.shape[1] // LANES):
        sl = slice(p * LANES, (p + 1) * LANES)
        qi_ref[:, sl] = _rope64(qi[:, sl], c64, s64, first_half).astype(qi_ref.dtype)

    m = idx_dot(wm_ref)
    roped = _rope64(m, c64, s64, first_half)
    xf = m + bf_ref[...]
    logsig = jnp.minimum(xf, 0.0) - jnp.log1p(jnp.exp(-jnp.abs(xf)))
    wsc = m * (H_IDX ** -0.5)
    ki_only = jnp.where(lane < MISC_F, roped, 0.0)
    misc_ref[...] = jnp.where(lane < MISC_F, roped,
                              jnp.where(lane < MISC_W, logsig,
                                        jnp.where(lane < MISC_END, wsc, 0.0)))
    kia_ref[...] = ki_only.astype(BF16)
    kib_ref[...] = pltpu.roll(ki_only, D_IDX, axis=1).astype(BF16)


def _row_specs(tm, d, rows_per_group, mod_rows):
    tiles_per_group = rows_per_group // tm
    x_spec = pl.BlockSpec((tm, d), lambda i: (i, 0))
    mod_spec = pl.BlockSpec((None, mod_rows, d), lambda i: (i // tiles_per_group, 0, 0))
    return x_spec, mod_spec


def _full(shape):
    nd = len(shape)
    return pl.BlockSpec(shape, lambda i: (0,) * nd)


def _proj_fox(x2d, sc, sh, wq, wk, wv, gq, gk, tm, rows_per_group):
    m, d = x2d.shape
    x_spec, mod_spec = _row_specs(tm, d, rows_per_group, sc.shape[1])
    n = wq.shape[1]
    o_spec = pl.BlockSpec((tm, n), lambda i: (i, 0))
    return pl.pallas_call(
        _proj_fox_kernel,
        grid=(m // tm,),
        in_specs=[x_spec, mod_spec, mod_spec, _full(wq.shape), _full(wk.shape), _full(wv.shape),
                  _full(gq.shape), _full(gk.shape)],
        out_specs=[o_spec] * 5,
        out_shape=[jax.ShapeDtypeStruct((m, n), BF16)] * 3 + [jax.ShapeDtypeStruct((m, n), F32)] * 2,
        compiler_params=_cp(("parallel",)),
        name="proj_fox",
    )(x2d, sc, sh, wq, wk, wv, gq, gk)


def _proj_dsa(x2d, sc, sh, wq, wk, wv, wi, wm, gq, gk, bf, tabs, tm, rows_per_group, tiles_per_seq):
    m, d = x2d.shape
    x_spec, mod_spec = _row_specs(tm, d, rows_per_group, sc.shape[1])
    tab_spec = pl.BlockSpec((tm, LANES), lambda i: (i % tiles_per_seq, 0))
    nq, nk, ni = wq.shape[1], wk.shape[1], wi.shape[1]

    def ospec(n):
        return pl.BlockSpec((tm, n), lambda i: (i, 0))

    def oshape(n, dt):
        return jax.ShapeDtypeStruct((m, n), dt)

    return pl.pallas_call(
        _proj_dsa_kernel,
        grid=(m // tm,),
        in_specs=[x_spec, mod_spec, mod_spec, _full(wq.shape), _full(wk.shape), _full(wv.shape),
                  _full(wi.shape), _full(wm.shape), _full(gq.shape), _full(gk.shape), _full(bf.shape),
                  tab_spec, tab_spec, tab_spec, tab_spec],
        out_specs=[ospec(nq), ospec(nk), ospec(nk), ospec(ni), ospec(LANES), ospec(LANES),
                   ospec(nk), ospec(nk), ospec(LANES)],
        out_shape=[oshape(nq, BF16), oshape(nk, BF16), oshape(nk, BF16), oshape(ni, wi.dtype),
                   oshape(LANES, BF16), oshape(LANES, BF16),
                   oshape(nk, F32), oshape(nk, F32), oshape(LANES, F32)],
        compiler_params=_cp(("parallel",)),
        name="proj_dsa",
    )(x2d, sc, sh, wq, wk, wv, wi, wm, gq, gk, bf, *tabs)


def _cumsum_kernel(x_ref, o_ref):
    x = x_ref[...]
    n = x.shape[1]
    lane = lax.broadcasted_iota(I32, x.shape, 1)
    k = 1
    while k < n:
        x = x + jnp.where(lane >= k, pltpu.roll(x, k, axis=1), 0.0)
        k *= 2
    o_ref[...] = x


def _cumsum_lanes(x):
    return pl.pallas_call(
        _cumsum_kernel,
        out_shape=jax.ShapeDtypeStruct(x.shape, F32),
        name="logf_cumsum",
    )(x)


def _fox_flash_kernel(q_ref, k_ref, v_ref, ck_ref, o_ref, s_ref, *, t, n_q):
    qi = pl.program_id(2)
    q = (q_ref[...].astype(F32) * (HEAD_DIM ** -0.5)).astype(BF16)

    def lane_fold(x, op, init):
        for j in range(x.shape[1] // LANES):
            init = op(init, x[:, j * LANES:(j + 1) * LANES])
        return init

    def attend(n_tiles):
        mx = jnp.full((t, LANES), -jnp.inf, F32)
        for c in range(n_tiles):
            s = _dot_nt(q, k_ref[c * t:(c + 1) * t, :]) - ck_ref[c]
            if c == n_tiles - 1:
                s = jnp.where(lax.broadcasted_iota(I32, s.shape, 1) <= lax.broadcasted_iota(I32, s.shape, 0),
                              s, -jnp.inf)
            s_ref[:, c * t:(c + 1) * t] = s
            mx = lane_fold(s, jnp.maximum, mx)
        m = jnp.max(mx, axis=-1, keepdims=True)
        ls = jnp.zeros((t, LANES), F32)
        acc = jnp.zeros((t, HEAD_DIM), F32)
        for c in range(n_tiles):
            p = jnp.exp(s_ref[:, c * t:(c + 1) * t] - m)
            ls = lane_fold(p, jnp.add, ls)
            acc = acc + _dot(p.astype(BF16), v_ref[c * t:(c + 1) * t, :])
        o_ref[...] = (acc / jnp.sum(ls, axis=-1, keepdims=True)).astype(o_ref.dtype)

    for n_tiles in range(1, n_q + 1):
        pl.when(qi == n_tiles - 1)(functools.partial(attend, n_tiles))


def _fox_prompt(qb, kb, vb, cum, t=512):
    b, s, hd = qb.shape
    h = hd // HEAD_DIM
    ck = cum.reshape(b, h, s // t, 1, t)
    kv_spec = pl.BlockSpec((None, s, HEAD_DIM), lambda bi, hi, qi: (bi, 0, hi))
    q_spec = pl.BlockSpec((None, t, HEAD_DIM), lambda bi, hi, qi: (bi, qi, hi))
    return pl.pallas_call(
        functools.partial(_fox_flash_kernel, t=t, n_q=s // t),
        grid=(b, h, s // t),
        in_specs=[q_spec, kv_spec, kv_spec,
                  pl.BlockSpec((None, None, s // t, 1, t), lambda bi, hi, qi: (bi, hi, 0, 0, 0))],
        out_specs=q_spec,
        out_shape=jax.ShapeDtypeStruct((b, s, hd), BF16),
        scratch_shapes=[pltpu.VMEM((t, s), F32)],
        compiler_params=_cp(("parallel", "parallel", "arbitrary")),
        name="fox_prompt",
    )(qb, kb, vb, ck)


def _order_key(x):
    bits = pltpu.bitcast(x + 0.0, I32)
    return jnp.where(bits < 0, bits ^ jnp.int32(0x7FFFFFFF), bits)


def _kth_largest_key(key, k, reduce_axes):
    shape = list(key.shape)
    for ax in reduce_axes:
        shape[ax] = 1
    sign = jnp.int32(-2 ** 31)

    def body(i, t):
        bit = lax.shift_left(jnp.int32(1), jnp.int32(31) - i)
        cand = t | bit
        ge = jnp.where(key >= (cand ^ sign), 1.0, 0.0)
        cnt = jnp.sum(ge, axis=reduce_axes, keepdims=True)
        return jnp.where(cnt >= float(k), cand, t)

    t = lax.fori_loop(0, 32, body, jnp.zeros(shape, I32))
    return t ^ sign


def _kth_largest_value(x, k, reduce_axes):
    kf = float(k)

    def count_ge(v):
        return jnp.sum(jnp.where(x >= v, 1.0, 0.0), axis=reduce_axes, keepdims=True)

    present = x > -jnp.inf
    top = jnp.max(x, axis=reduce_axes, keepdims=True)
    lo0 = jnp.min(jnp.where(present, x, jnp.inf), axis=reduce_axes, keepdims=True)

    def body(i, carry):
        lo, hi = carry
        mid = lo + 0.5 * (hi - lo)
        ge = count_ge(mid) >= kf
        return jnp.where(ge, mid, lo), jnp.where(ge, hi, mid)

    _, hi = lax.fori_loop(0, BISECT_STEPS, body, (lo0, top))
    below = jnp.max(jnp.where(x < hi, x, -jnp.inf), axis=reduce_axes, keepdims=True)
    thr = jnp.where(count_ge(top) >= kf, top, below)
    n_present = jnp.sum(jnp.where(present, 1.0, 0.0), axis=reduce_axes, keepdims=True)
    return jnp.where(n_present < kf, -jnp.inf, thr)


def _dsa_prompt_kernel(qi_ref, kia_ref, kib_ref, misc_ref, qd_ref, kd_ref, vd_ref, o_ref,
                       score_ref, selb_ref, *, tq, n_sel, q0):
    s_len = kia_ref.shape[0]
    row = (pl.program_id(1) + q0) * tq + lax.broadcasted_iota(I32, (tq, s_len), 0)
    col = lax.broadcasted_iota(I32, (tq, s_len), 1)
    causal = col <= row

    misc = misc_ref[...]
    kia, kib = kia_ref[...], kib_ref[...]
    acc = jnp.zeros((tq, s_len), F32)
    for p in range(qi_ref.shape[1] // LANES):
        qblk = qi_ref[:, p * LANES:(p + 1) * LANES]
        for half, kmat in enumerate((kia, kib)):
            hh = 2 * p + half
            w = misc[:, MISC_W + hh:MISC_W + hh + 1] * (D_IDX ** -0.5)
            acc = acc + w * jnp.maximum(_dot_nt(qblk, kmat), 0.0)
    score_ref[...] = jnp.where(causal, acc, -jnp.inf)

    def count(mask):
        return jnp.sum(jnp.where(mask, 1.0, 0.0), axis=1, keepdims=True)

    def select_ranked(vals, thr):
        need = float(n_sel) - count(vals > thr)
        ch = 256
        r_i = lax.broadcasted_iota(I32, (ch, ch), 0)
        c_i = lax.broadcasted_iota(I32, (ch, ch), 1)
        tri = jnp.where(r_i < c_i, 1.0, 0.0).astype(BF16)
        base = jnp.zeros((tq, 1), F32)
        for c in range(s_len // ch):
            sl = slice(c * ch, (c + 1) * ch)
            eq_c = jnp.where(vals[:, sl] == thr, 1.0, 0.0)
            rank = base + _dot(eq_c.astype(BF16), tri)
            take = (vals[:, sl] > thr) | ((eq_c > 0.0) & (rank < need))
            selb_ref[:, sl] = jnp.where(take & causal[:, sl], 0.0, -jnp.inf)
            base = base + jnp.sum(eq_c, axis=1, keepdims=True)

    score = score_ref[...]
    thr = _kth_largest_value(score, n_sel, (1,))
    cnt_ge = count(score >= thr)
    selb_ref[...] = jnp.where((score >= thr) & causal, 0.0, -jnp.inf)
    finite_thr = thr > -jnp.inf
    tie_rows = jnp.where((cnt_ge > float(n_sel)) & finite_thr, 1.0, 0.0)
    unresolved = jnp.where((cnt_ge < float(n_sel)) & finite_thr, 1.0, 0.0)

    @pl.when(jnp.max(tie_rows) > 0.0)
    def _():
        select_ranked(score, thr)

    @pl.when(jnp.max(unresolved) > 0.0)
    def _():
        key = _order_key(score)
        select_ranked(key, _kth_largest_key(key, n_sel, (1,)))

    n_heads = qd_ref.shape[1] // HEAD_DIM
    rep = n_heads // KV_DSA
    for hh in range(n_heads):
        g = hh // rep
        q = qd_ref[:, hh * HEAD_DIM:(hh + 1) * HEAD_DIM]
        kg = kd_ref[:, g * HEAD_DIM:(g + 1) * HEAD_DIM]
        vg = vd_ref[:, g * HEAD_DIM:(g + 1) * HEAD_DIM]
        s = _dot_nt(q, kg) * (HEAD_DIM ** -0.5) + selb_ref[...]
        m = jnp.max(s, axis=-1, keepdims=True)
        p = jnp.exp(s - m)
        l = jnp.sum(p, axis=-1, keepdims=True)
        o_ref[:, hh * HEAD_DIM:(hh + 1) * HEAD_DIM] = (_dot(p.astype(BF16), vg) / l).astype(o_ref.dtype)


def _dsa_prompt(qi, kia, kib, misc, qd, kd, vd, tq=256, tiles_per_call=2):
    b, s, _ = qi.shape
    n_sel = min(TOPK_MAX, s // 4)
    outs = []
    for q0 in range(0, s // tq, tiles_per_call):
        s_eff = (q0 + tiles_per_call) * tq

        def qspec(n, q0=q0):
            return pl.BlockSpec((None, tq, n), lambda bi, i: (bi, i + q0, 0))

        def kspec(n, s_eff=s_eff):
            return pl.BlockSpec((None, s_eff, n), lambda bi, i: (bi, 0, 0))

        outs.append(pl.pallas_call(
            functools.partial(_dsa_prompt_kernel, tq=tq, n_sel=n_sel, q0=q0),
            grid=(b, tiles_per_call),
            in_specs=[qspec(qi.shape[2]), kspec(LANES), kspec(LANES), qspec(LANES),
                      qspec(qd.shape[2]), kspec(kd.shape[2]), kspec(vd.shape[2])],
            out_specs=pl.BlockSpec((None, tq, qd.shape[2]), lambda bi, i: (bi, i, 0)),
            out_shape=jax.ShapeDtypeStruct((b, tiles_per_call * tq, qd.shape[2]), BF16),
            scratch_shapes=[pltpu.VMEM((tq, s_eff), F32), pltpu.VMEM((tq, s_eff), F32)],
            compiler_params=_cp(("parallel", "parallel")),
            name="dsa_prompt",
        )(qi, kia, kib, misc, qd, kd, vd))
    return jnp.concatenate(outs, axis=1)


def _idx_score_rows(qi, w, kt_refs):
    rows = []
    for kt_ref in kt_refs:
        d = jnp.dot(qi, kt_ref[...], precision=lax.Precision.HIGHEST, preferred_element_type=F32)
        rows.append(jnp.sum(w * jnp.maximum(d, 0.0), axis=0, keepdims=True))
    return jnp.concatenate(rows, axis=0)


def _fox_sample_kernel(pt_ref, q_ref, kn_ref, vn_ref, lfn_ref, qi_ref, wi_ref, kin_ref, *refs, pp, n_heads):
    k_refs = refs[:pp]
    v_refs = refs[pp:2 * pp]
    lf_refs = refs[2 * pp:3 * pp]
    kt_refs = refs[3 * pp:4 * pp]
    o_ref, sc_ref, scn_ref, m_ref, l_ref, acc_ref, carry_ref = refs[4 * pp:]
    c = pl.program_id(1)
    width = PAGE_SIZE * n_heads
    q = q_ref[...] * (HEAD_DIM ** -0.5)
    qb = q.astype(BF16)
    qi = qi_ref[...]
    wi = wi_ref[...] * (D_IDX ** -0.5)
    own_head = (lax.broadcasted_iota(I32, (n_heads, width), 1) % n_heads
                == lax.broadcasted_iota(I32, (n_heads, width), 0))

    @pl.when(c == 0)
    def _():
        s_new = jnp.sum(q * kn_ref[...], axis=-1, keepdims=True)
        m_ref[...] = jnp.broadcast_to(s_new, m_ref.shape)
        l_ref[...] = jnp.ones(l_ref.shape, F32)
        acc_ref[...] = vn_ref[...]
        carry_ref[...] = lfn_ref[...]
        d = jnp.sum(qi * kin_ref[...], axis=-1, keepdims=True)
        sn = jnp.sum(wi * jnp.maximum(d, 0.0), axis=0, keepdims=True)
        r = lax.broadcasted_iota(I32, scn_ref.shape, 0)
        ln = lax.broadcasted_iota(I32, scn_ref.shape, 1)
        scn_ref[...] = jnp.where((r == 0) & (ln == 0), jnp.broadcast_to(sn, scn_ref.shape), -jnp.inf)

    sc_ref[...] = _idx_score_rows(qi, wi, kt_refs)

    lf = jnp.concatenate([r[...] for r in lf_refs], axis=0)
    lane = lax.broadcasted_iota(I32, lf.shape, 1)
    tot, suf = lf, lf
    k = n_heads
    while k < width:
        tot = tot + pltpu.roll(tot, k, axis=1)
        suf = suf + jnp.where(lane + k < width, pltpu.roll(suf, width - k, axis=1), 0.0)
        k *= 2
    later = suf - lf
    run = carry_ref[...]
    bias = [None] * pp
    for j in reversed(range(pp)):
        bias[j] = run + later[j:j + 1, :]
        run = run + tot[j:j + 1, :]
    carry_ref[...] = run

    logits = []
    for j in range(pp):
        s = _dot_nt(qb, k_refs[j][...].astype(BF16))
        logits.append(jnp.where(own_head, s + bias[j], NEG))
    m_prev = m_ref[...]
    m_new = m_prev
    for j in range(pp):
        m_new = jnp.maximum(m_new, jnp.max(logits[j], axis=-1, keepdims=True))
    alpha = jnp.exp(m_prev - m_new)
    l_new = alpha * l_ref[...]
    acc = alpha * acc_ref[...]
    for j in range(pp):
        p = jnp.exp(logits[j] - m_new[:, :1])
        l_new = l_new + jnp.sum(p, axis=-1, keepdims=True)
        acc = acc + _dot(p.astype(BF16), v_refs[j][...].astype(BF16))
    m_ref[...] = m_new
    l_ref[...] = l_new
    acc_ref[...] = acc

    @pl.when(c == pl.num_programs(1) - 1)
    def _():
        o_ref[...] = acc / l_new


def _fox_sample(page_table, q, k_new, v_new, lf_new, qi, wi, ki_new, cache_k, cache_v, cache_lf, cache_idx_t, pp=8):
    db, n_heads, _ = q.shape
    n_pages = page_table.shape[1]
    nc = n_pages // pp
    width = PAGE_SIZE * n_heads

    def page_idx(j):
        return lambda bi, ci, pt: (pt[bi, (nc - 1 - ci) * pp + j], 0, 0)

    def per_seq(shape):
        return pl.BlockSpec((None,) + shape, lambda bi, ci, pt: (bi, 0, 0))

    vec_spec = per_seq((n_heads, HEAD_DIM))
    kv_specs = [pl.BlockSpec((None, width, HEAD_DIM), page_idx(j)) for j in range(pp)]
    lf_specs = [pl.BlockSpec((None, 1, width), page_idx(j)) for j in range(pp)]
    kt_specs = [pl.BlockSpec((None, D_IDX, PAGE_SIZE), page_idx(j)) for j in range(pp)]
    grid_spec = pltpu.PrefetchScalarGridSpec(
        num_scalar_prefetch=1,
        grid=(db, nc),
        in_specs=[vec_spec, vec_spec, vec_spec, per_seq((1, width)),
                  per_seq((H_IDX, D_IDX)), per_seq((H_IDX, LANES)), per_seq((1, D_IDX))]
                 + kv_specs + kv_specs + lf_specs + kt_specs,
        out_specs=[vec_spec,
                   pl.BlockSpec((None, pp, PAGE_SIZE), lambda bi, ci, pt: (bi, nc - 1 - ci, 0)),
                   per_seq((8, LANES))],
        scratch_shapes=[pltpu.VMEM((n_heads, HEAD_DIM), F32)] * 3 + [pltpu.VMEM((1, width), F32)],
    )
    return pl.pallas_call(
        functools.partial(_fox_sample_kernel, pp=pp, n_heads=n_heads),
        grid_spec=grid_spec,
        out_shape=[jax.ShapeDtypeStruct((db, n_heads, HEAD_DIM), F32),
                   jax.ShapeDtypeStruct((db, n_pages, PAGE_SIZE), F32),
                   jax.ShapeDtypeStruct((db, 8, LANES), F32)],
        compiler_params=_cp(("parallel", "arbitrary")),
        name="fox_sample",
    )(page_table, q, k_new, v_new, lf_new, qi, wi, ki_new,
      *([cache_k] * pp), *([cache_v] * pp), *([cache_lf] * pp), *([cache_idx_t] * pp))


def _select_kernel(s_ref, past_ref, new_ref, take_ref, *, n_sel, n_pages):
    score = s_ref[...]
    db, rows, _ = score.shape
    valid = score > -jnp.inf

    def count(mask):
        return jnp.sum(jnp.where(mask, 1.0, 0.0), axis=(1, 2), keepdims=True)

    def select_ranked(vals, thr):
        need = float(n_sel) - count(vals > thr)
        eq = jnp.where(vals == thr, 1.0, 0.0)
        r_i = lax.broadcasted_iota(I32, (LANES, LANES), 0)
        c_i = lax.broadcasted_iota(I32, (LANES, LANES), 1)
        tri = jnp.where(r_i < c_i, 1.0, 0.0).astype(BF16)
        within = _dot(eq.reshape(db * rows, LANES).astype(BF16), tri).reshape(db, rows, LANES)
        row_cnt = jnp.broadcast_to(jnp.sum(eq, axis=2, keepdims=True), score.shape)
        rows_pad = 2 * LANES
        cnt_pad = jnp.concatenate([row_cnt, jnp.zeros((db, rows_pad - rows, LANES), F32)], axis=1)
        rr = lax.broadcasted_iota(I32, (db, rows_pad, rows_pad), 1)
        rc = lax.broadcasted_iota(I32, (db, rows_pad, rows_pad), 2)
        before = jnp.einsum("bij,bjl->bil", jnp.where(rc < rr, 1.0, 0.0).astype(BF16), cnt_pad.astype(BF16),
                            preferred_element_type=F32)[:, :rows, :]
        take = (vals > thr) | ((eq > 0.0) & (within + before < need))
        take_ref[...] = jnp.where(take & valid, 1.0, 0.0)

    thr = _kth_largest_value(score, n_sel, (1, 2))
    cnt_ge = count(score >= thr)
    take_ref[...] = jnp.where((score >= thr) & valid, 1.0, 0.0)
    finite_thr = thr > -jnp.inf
    ties = jnp.where((cnt_ge > float(n_sel)) & finite_thr, 1.0, 0.0)
    unresolved = jnp.where((cnt_ge < float(n_sel)) & finite_thr, 1.0, 0.0)

    @pl.when(jnp.max(ties) > 0.0)
    def _():
        select_ranked(score, thr)

    @pl.when(jnp.max(unresolved) > 0.0)
    def _():
        key = _order_key(score)
        select_ranked(key, _kth_largest_key(key, n_sel, (1, 2)))

    take01 = take_ref[...]
    new_ref[...] = jnp.where(take01[:, n_pages:, :] > 0.5, 0.0, NEG)
    t_i = lax.broadcasted_iota(I32, (LANES, LANES * KV_DSA), 0)
    l_i = lax.broadcasted_iota(I32, (LANES, LANES * KV_DSA), 1)
    spread = jnp.where(l_i // KV_DSA == t_i, 1.0, 0.0).astype(BF16)
    past = _dot(take01[:, :n_pages, :].reshape(db * n_pages, LANES).astype(BF16), spread)
    past_ref[...] = jnp.where(past > 0.5, 0.0, NEG).reshape(db, n_pages, LANES * KV_DSA)


def _select_sample(scores, n_sel, n_pages):
    db, rows, lanes = scores.shape
    return pl.pallas_call(
        functools.partial(_select_kernel, n_sel=n_sel, n_pages=n_pages),
        out_shape=[jax.ShapeDtypeStruct((db, n_pages, lanes * KV_DSA), F32),
                   jax.ShapeDtypeStruct((db, rows - n_pages, lanes), F32)],
        scratch_shapes=[pltpu.VMEM((db, rows, lanes), F32)],
        compiler_params=pltpu.CompilerParams(vmem_limit_bytes=VMEM_LIMIT),
        name="select_sample",
    )(scores)


def _dsa_sample_kernel(pt_ref, q_ref, kn_ref, vn_ref, selb_ref, selbn_ref, *refs, pp, n_heads):
    k_refs = refs[:pp]
    v_refs = refs[pp:2 * pp]
    o_ref, m_ref, l_ref, acc_ref = refs[2 * pp:]
    c = pl.program_id(1)
    rep = n_heads // KV_DSA
    width = PAGE_SIZE * KV_DSA
    q = q_ref[...] * (HEAD_DIM ** -0.5)
    qb = q.astype(BF16)
    grp = lax.broadcasted_iota(I32, (n_heads, HEAD_DIM), 0) // rep
    own_kv = (lax.broadcasted_iota(I32, (n_heads, width), 1) % KV_DSA
              == lax.broadcasted_iota(I32, (n_heads, width), 0) // rep)

    @pl.when(c == 0)
    def _():
        kn = kn_ref[...]
        vn = vn_ref[...]
        s_new = jnp.zeros((n_heads, 1), F32)
        v_rows = jnp.zeros((n_heads, HEAD_DIM), F32)
        for g in range(KV_DSA):
            sg = jnp.sum(q * kn[g:g + 1, :], axis=-1, keepdims=True)
            s_new = jnp.where(grp[:, :1] == g, sg, s_new)
            v_rows = jnp.where(grp == g, jnp.broadcast_to(vn[g:g + 1, :], v_rows.shape), v_rows)
        s_new = s_new + selbn_ref[0:1, 0:1]
        m_ref[...] = jnp.broadcast_to(jnp.maximum(s_new, NEG), m_ref.shape)
        p_new = jnp.exp(s_new - m_ref[:, :1])
        l_ref[...] = jnp.broadcast_to(p_new, l_ref.shape)
        acc_ref[...] = p_new * v_rows

    selb = selb_ref[...]
    logits = []
    for j in range(pp):
        s = _dot_nt(qb, k_refs[j][...].astype(BF16))
        logits.append(jnp.where(own_kv, s + selb[j:j + 1, :], NEG))

    m_prev = m_ref[...]
    m_new = m_prev
    for j in range(pp):
        m_new = jnp.maximum(m_new, jnp.max(logits[j], axis=-1, keepdims=True))
    alpha = jnp.exp(m_prev - m_new)
    l_new = alpha * l_ref[...]
    acc = alpha * acc_ref[...]
    for j in range(pp):
        p = jnp.exp(logits[j] - m_new[:, :1])
        l_new = l_new + jnp.sum(p, axis=-1, keepdims=True)
        acc = acc + _dot(p.astype(BF16), v_refs[j][...].astype(BF16))
    m_ref[...] = m_new
    l_ref[...] = l_new
    acc_ref[...] = acc

    @pl.when(c == pl.num_programs(1) - 1)
    def _():
        o_ref[...] = acc / l_new


def _dsa_sample(page_table, q, k_new, v_new, selb_past, selb_new, cache_k, cache_v, pp=16):
    db, n_heads, _ = q.shape
    n_pages = page_table.shape[1]
    nc = n_pages // pp
    width = PAGE_SIZE * KV_DSA

    def page_idx(j):
        return lambda bi, ci, pt: (pt[bi, ci * pp + j], 0, 0)

    q_spec = pl.BlockSpec((None, n_heads, HEAD_DIM), lambda bi, ci, pt: (bi, 0, 0))
    n_spec = pl.BlockSpec((None, KV_DSA, HEAD_DIM), lambda bi, ci, pt: (bi, 0, 0))
    kv_specs = [pl.BlockSpec((None, width, HEAD_DIM), page_idx(j)) for j in range(pp)]
    grid_spec = pltpu.PrefetchScalarGridSpec(
        num_scalar_prefetch=1,
        grid=(db, nc),
        in_specs=[q_spec, n_spec, n_spec,
                  pl.BlockSpec((None, pp, width), lambda bi, ci, pt: (bi, ci, 0)),
                  pl.BlockSpec((None,) + selb_new.shape[1:], lambda bi, ci, pt: (bi, 0, 0))]
                 + kv_specs + kv_specs,
        out_specs=q_spec,
        scratch_shapes=[pltpu.VMEM((n_heads, HEAD_DIM), F32)] * 3,
    )
    return pl.pallas_call(
        functools.partial(_dsa_sample_kernel, pp=pp, n_heads=n_heads),
        grid_spec=grid_spec,
        out_shape=jax.ShapeDtypeStruct((db, n_heads, HEAD_DIM), F32),
        compiler_params=_cp(("parallel", "arbitrary")),
        name="dsa_sample",
    )(page_table, q, k_new, v_new, selb_past, selb_new, *([cache_k] * pp), *([cache_v] * pp))


def _outproj_kernel(x_ref, of_ref, od_ref, wf_ref, wd_ref, ga_ref, sc_ref, sh_ref, wr_ref,
                    x1_ref, h2_ref, route_ref):
    def proj(a_ref, w_ref):
        if w_ref.dtype == F32:
            return jnp.dot(a_ref[...].astype(F32), w_ref[...], precision=lax.Precision.HIGHEST,
                           preferred_element_type=F32)
        return _dot(a_ref[...].astype(BF16), w_ref[...])

    mix = proj(of_ref, wf_ref) + proj(od_ref, wd_ref)
    x1 = x_ref[...] + ga_ref[...] * mix
    x1_ref[...] = x1
    h2 = _modulated(x1, sc_ref, sh_ref)
    h2_ref[...] = h2
    if wr_ref.dtype == F32:
        logits = jnp.dot(h2, wr_ref[...], precision=lax.Precision.HIGHEST, preferred_element_type=F32)
    else:
        logits = _dot(h2.astype(BF16), wr_ref[...])
    lane = lax.broadcasted_iota(I32, logits.shape, 1).astype(F32)

    def first_argmax(vals, mask):
        v = jnp.where(mask, vals, -jnp.inf)
        mx = jnp.max(v, axis=-1, keepdims=True)
        idx = jnp.min(jnp.where(v == mx, lane, float(LANES)), axis=-1, keepdims=True)
        return mx, idx

    is_g = lane < float(N_GROUPS)
    mg, g_sel = first_argmax(logits, is_g)
    p_gsel = 1.0 / jnp.sum(jnp.where(is_g, jnp.exp(logits - mg), 0.0), axis=-1, keepdims=True)
    lo = float(N_GROUPS) + g_sel * float(EXPERTS_PER_GROUP)
    in_grp = (lane >= lo) & (lane < lo + float(EXPERTS_PER_GROUP))
    m1, i1 = first_argmax(logits, in_grp)
    m2, i2 = first_argmax(logits, in_grp & (lane != i1))
    e2 = jnp.exp(m2 - m1)
    w1 = p_gsel / (1.0 + e2)
    w2 = p_gsel * e2 / (1.0 + e2)
    route_ref[...] = jnp.where(lane == 0.0, i1 - float(N_GROUPS),
                               jnp.where(lane == 1.0, i2 - float(N_GROUPS),
                                         jnp.where(lane == 2.0, w1, jnp.where(lane == 3.0, w2, 0.0))))


def _outproj(x2d, o_fox, o_dsa, w_out, ga, sc, sh, wr, tm, rows_per_group):
    m, d = x2d.shape
    x_spec, mod_spec = _row_specs(tm, d, rows_per_group, sc.shape[1])
    half = o_fox.shape[1]
    half_spec = pl.BlockSpec((tm, half), lambda i: (i, 0))
    wf, wd = w_out, w_out
    return pl.pallas_call(
        _outproj_kernel,
        grid=(m // tm,),
        in_specs=[x_spec, half_spec, half_spec,
                  pl.BlockSpec((half, d), lambda i: (0, 0)), pl.BlockSpec((half, d), lambda i: (1, 0)),
                  mod_spec, mod_spec, mod_spec, _full(wr.shape)],
        out_specs=[x_spec, x_spec, pl.BlockSpec((tm, LANES), lambda i: (i, 0))],
        out_shape=[jax.ShapeDtypeStruct((m, d), F32), jax.ShapeDtypeStruct((m, d), F32),
                   jax.ShapeDtypeStruct((m, LANES), F32)],
        compiler_params=_cp(("parallel",)),
        name="outproj_router",
    )(x2d, o_fox, o_dsa, wf, wd, ga, sc, sh, wr)


def _dispatch_kernel(ends_ref, s0_ref, s1_ref, h_ref, t0_ref, t1_ref, ht_ref, xs_hbm, zbuf, sem, zsem,
                     *, tm, n_main):
    i = pl.program_id(0)

    @pl.when(i == 0)
    def _():
        zbuf[...] = jnp.zeros(zbuf.shape, F32)

        def fill(e):
            start = pl.multiple_of(jnp.maximum(ends_ref[0, e] - tm, 0), tm)
            return pltpu.make_async_copy(zbuf, xs_hbm.at[pl.ds(start, tm)], zsem)

        for e in range(N_EXPERTS):
            fill(e).start()
        for e in range(N_EXPERTS):
            fill(e).wait()

        def clear_unused(k, carry):
            cp = pltpu.make_async_copy(zbuf, xs_hbm.at[pl.ds(pl.multiple_of(k * tm, tm), tm)], zsem)
            cp.start()
            cp.wait()
            return carry

        lax.fori_loop(ends_ref[0, N_EXPERTS - 1] // tm, xs_hbm.shape[0] // tm, clear_unused, 0)

    def scatter(a_ref, b_ref, src_ref, rows):
        def issue(r, carry):
            src = src_ref.at[pl.ds(r, 1)]
            pltpu.make_async_copy(src, xs_hbm.at[pl.ds(a_ref[0, r], 1)], sem).start()
            pltpu.make_async_copy(src, xs_hbm.at[pl.ds(b_ref[0, r], 1)], sem).start()
            return carry

        lax.fori_loop(0, rows, issue, 0, unroll=8)
        pltpu.make_async_copy(src_ref, xs_hbm.at[pl.ds(0, rows)], sem).wait()
        pltpu.make_async_copy(src_ref, xs_hbm.at[pl.ds(0, rows)], sem).wait()

    @pl.when(i < n_main)
    def _():
        scatter(s0_ref, s1_ref, h_ref, tm)

    @pl.when(i == n_main)
    def _():
        scatter(t0_ref, t1_ref, ht_ref, ht_ref.shape[0])


def _dispatch(ends, slots_main, h_main, slots_tail, h_tail, n_slots, tm):
    m, d = h_main.shape
    mt = h_tail.shape[0]
    n_main = m // tm
    smem = pltpu.SMEM

    def main_idx(i):
        return (jnp.minimum(i, n_main - 1), 0, 0)

    id_spec = pl.BlockSpec((None, 1, tm), main_idx, memory_space=smem)
    tail_spec = pl.BlockSpec((None, 1, mt), lambda i: (0, 0, 0), memory_space=smem)
    return pl.pallas_call(
        functools.partial(_dispatch_kernel, tm=tm, n_main=n_main),
        grid=(n_main + 1,),
        in_specs=[pl.BlockSpec(memory_space=smem), id_spec, id_spec,
                  pl.BlockSpec((tm, d), lambda i: (jnp.minimum(i, n_main - 1), 0)),
                  tail_spec, tail_spec, pl.BlockSpec((mt, d), lambda i: (0, 0))],
        out_specs=pl.BlockSpec(memory_space=pl.ANY),
        out_shape=jax.ShapeDtypeStruct((n_slots, d), F32),
        scratch_shapes=[pltpu.VMEM((tm, d), F32), pltpu.SemaphoreType.DMA(()), pltpu.SemaphoreType.DMA(())],
        compiler_params=_cp(("arbitrary",)),
        name="moe_dispatch",
    )(ends.reshape(1, N_EXPERTS),
      slots_main[:, 0].reshape(n_main, 1, tm), slots_main[:, 1].reshape(n_main, 1, tm), h_main,
      slots_tail[:, 0].reshape(1, 1, mt), slots_tail[:, 1].reshape(1, 1, mt), h_tail)


def _moe_kernel(te_ref, na_ref, x_ref, wg_ref, wu_ref, wd_ref, o_ref, wg_b, wu_b, wd_b):
    t = pl.program_id(0)

    @pl.when(t < na_ref[0])
    def _():
        @pl.when((t == 0) | (te_ref[t] != te_ref[jnp.maximum(t - 1, 0)]))
        def _():
            wg_b[...] = wg_ref[...].astype(BF16)
            wu_b[...] = wu_ref[...].astype(BF16)
            wd_b[...] = wd_ref[...].astype(BF16)

        h = x_ref[...].astype(BF16)
        a = _dot(h, wg_b[...])
        u = _dot(h, wu_b[...])
        act = (a / (1.0 + jnp.exp(-a))) * u
        o_ref[...] = _dot(act.astype(BF16), wd_b[...])

    @pl.when(t >= na_ref[0])
    def _():
        o_ref[...] = jnp.zeros(o_ref.shape, F32)


def _moe_experts(tile_expert, n_active, x_sorted, w_gate, w_up, w_down, tm):
    n_tiles = tile_expert.shape[0]
    d = x_sorted.shape[1]
    f = w_gate.shape[2]

    def row_idx(t, te, na):
        return (jnp.minimum(t, na[0] - 1), 0)

    def w_idx(t, te, na):
        return (te[jnp.minimum(t, na[0] - 1)], 0, 0)

    grid_spec = pltpu.PrefetchScalarGridSpec(
        num_scalar_prefetch=2,
        grid=(n_tiles,),
        in_specs=[pl.BlockSpec((tm, d), row_idx),
                  pl.BlockSpec((None, d, f), w_idx),
                  pl.BlockSpec((None, d, f), w_idx),
                  pl.BlockSpec((None, f, d), w_idx)],
        out_specs=pl.BlockSpec((tm, d), lambda t, te, na: (t, 0)),
        scratch_shapes=[pltpu.VMEM((d, f), BF16), pltpu.VMEM((d, f), BF16), pltpu.VMEM((f, d), BF16)],
    )
    return pl.pallas_call(
        _moe_kernel,
        grid_spec=grid_spec,
        out_shape=jax.ShapeDtypeStruct((n_tiles * tm, d), F32),
        compiler_params=_cp(("arbitrary",)),
        name="moe_experts",
    )(tile_expert, n_active, x_sorted, w_gate, w_up, w_down)


def _combine_kernel(s0_ref, s1_ref, x1_ref, gm_ref, route_ref, y_hbm, o_ref, b0, b1, sem, *, tm):
    def copies(r):
        return (pltpu.make_async_copy(y_hbm.at[pl.ds(s0_ref[0, r], 1)], b0.at[pl.ds(r, 1)], sem),
                pltpu.make_async_copy(y_hbm.at[pl.ds(s1_ref[0, r], 1)], b1.at[pl.ds(r, 1)], sem))

    def issue(r, carry):
        c0, c1 = copies(r)
        c0.start()
        c1.start()
        return carry

    lax.fori_loop(0, tm, issue, 0, unroll=8)
    pltpu.make_async_copy(y_hbm.at[pl.ds(0, tm)], b0, sem).wait()
    pltpu.make_async_copy(y_hbm.at[pl.ds(0, tm)], b1, sem).wait()
    route = route_ref[...]
    moe = route[:, 2:3] * b0[...] + route[:, 3:4] * b1[...]
    o_ref[...] = x1_ref[...] + gm_ref[...] * moe


def _combine(slot0, slot1, x1, gm, route, y_slots, tm, rows_per_group):
    m, d = x1.shape
    x_spec, mod_spec = _row_specs(tm, d, rows_per_group, gm.shape[1])
    n_tiles = m // tm
    id_spec = pl.BlockSpec((None, 1, tm), lambda i: (i, 0, 0), memory_space=pltpu.SMEM)
    return pl.pallas_call(
        functools.partial(_combine_kernel, tm=tm),
        grid=(n_tiles,),
        in_specs=[id_spec, id_spec, x_spec, mod_spec,
                  pl.BlockSpec((tm, LANES), lambda i: (i, 0)),
                  pl.BlockSpec(memory_space=pl.ANY)],
        out_specs=x_spec,
        out_shape=jax.ShapeDtypeStruct((m, d), F32),
        scratch_shapes=[pltpu.VMEM((tm, d), F32), pltpu.VMEM((tm, d), F32), pltpu.SemaphoreType.DMA(())],
        compiler_params=_cp(("arbitrary",)),
        name="moe_combine",
    )(slot0.reshape(n_tiles, 1, tm), slot1.reshape(n_tiles, 1, tm), x1, gm, route, y_slots)


def _moe_plan(expert_ids, tm):
    t_all = expert_ids.shape[0]
    flat = expert_ids.reshape(-1)
    onehot = (flat[:, None] == jnp.arange(N_EXPERTS, dtype=I32)[None, :]).astype(I32)
    rank = jnp.sum((jnp.cumsum(onehot, axis=0) - onehot) * onehot, axis=1)
    counts = jnp.sum(onehot, axis=0)
    padded = ((counts + tm - 1) // tm) * tm
    ends = jnp.cumsum(padded).astype(I32)
    starts = ends - padded
    slot = jnp.sum(onehot * starts[None, :], axis=1) + rank
    n_tiles = (2 * t_all + N_EXPERTS * (tm - 1)) // tm + 1
    tile_start = jnp.arange(n_tiles, dtype=I32) * tm
    tile_expert = jnp.minimum(jnp.sum((tile_start[:, None] >= ends[None, :]).astype(I32), axis=1),
                              N_EXPERTS - 1)
    n_active = (ends[-1:] // tm).astype(I32)
    return tile_expert.astype(I32), n_active, ends, slot.reshape(t_all, 2).astype(I32)


def _rope_tables(pos, dim, lanes):
    half = dim // 2
    inv = ROPE_THETA ** (-jnp.arange(half, dtype=F32) / half)
    ang = pos.astype(F32)[:, None] * inv[None, :]
    cos, sin = jnp.cos(ang), jnp.sin(ang)
    reps = lanes // dim
    cos_t = jnp.tile(jnp.concatenate([cos, cos], axis=-1), (1, reps))
    sin_t = jnp.tile(jnp.concatenate([-sin, sin], axis=-1), (1, reps))
    return cos_t, sin_t


def kernel(x_prompt, x_sample, cache_fox_k, cache_fox_v, cache_fox_logf, cache_dsa_k, cache_dsa_v,
           cache_idx_k, page_table, c_prompt, c_sample, w_in, b_forget, q_gain_fox, k_gain_fox,
           q_gain_dsa, k_gain_dsa, w_out, w_ada, b_ada, w_router_group, w_router_expert,
           w_gate, w_up, w_down):
    b, s, d = x_prompt.shape
    db, dq, _ = x_sample.shape
    depth = w_in.shape[0]
    assert depth == 1 and dq == 1
    past_len = page_table.shape[1] * PAGE_SIZE
    n_phys = cache_fox_k.shape[1]
    h_fox = cache_fox_k.shape[3]
    h_dsa = d // (2 * HEAD_DIM)
    t_p, t_s = b * s, db * dq
    t_all = t_p + t_s
    tm_p, tm_s = 256, t_s
    layer = 0

    n_c = b + db
    pad = (-n_c) % 8
    c_all = jnp.concatenate([c_prompt, c_sample, jnp.zeros((pad, d), F32)], axis=0)
    mod = _adaln(c_all, w_ada[layer], b_ada[layer])
    mods = [mod[:, i * d:(i + 1) * d] for i in range(6)]
    mod_p = [m_[:b].reshape(b, 1, d) for m_ in mods]
    mod_s = [m_[b:b + db].reshape(1, db, d) for m_ in mods]

    w = w_in[layer]
    nf = h_fox * HEAD_DIM
    nd = h_dsa * HEAD_DIM
    nkv = KV_DSA * HEAD_DIM
    ni = H_IDX * D_IDX
    o = 0
    wqf = w[:, o:o + nf].astype(BF16); o += nf
    wkf = w[:, o:o + nf].astype(BF16); o += nf
    wvf = w[:, o:o + nf].astype(BF16); o += nf
    w_fl = w[:, o:o + h_fox]; o += h_fox
    wqd = w[:, o:o + nd].astype(BF16); o += nd
    wkd = w[:, o:o + nkv].astype(BF16); o += nkv
    wvd = w[:, o:o + nkv].astype(BF16); o += nkv
    wqi = w[:, o:o + ni]; o += ni
    w_ki = w[:, o:o + D_IDX]; o += D_IDX
    w_wi = w[:, o:o + H_IDX]; o += H_IDX
    assert h_fox <= MISC_W - MISC_F
    wmisc = jnp.concatenate([w_ki, w_fl, jnp.zeros((d, MISC_W - MISC_F - h_fox), F32), w_wi,
                             jnp.zeros((d, LANES - MISC_END), F32)], axis=1)
    bf = jnp.zeros((1, LANES), F32).at[0, MISC_F:MISC_F + h_fox].set(b_forget[layer])
    gqf, gkf = q_gain_fox[layer].reshape(1, HEAD_DIM), k_gain_fox[layer].reshape(1, HEAD_DIM)
    gqd, gkd = q_gain_dsa[layer].reshape(1, HEAD_DIM), k_gain_dsa[layer].reshape(1, HEAD_DIM)

    pos_p = jnp.arange(s, dtype=I32)
    pos_s = jnp.full((t_s,), past_len, I32)
    tabs_p = _rope_tables(pos_p, HEAD_DIM, LANES) + _rope_tables(pos_p, D_IDX, LANES)
    tabs_s = _rope_tables(pos_s, HEAD_DIM, LANES) + _rope_tables(pos_s, D_IDX, LANES)

    xp2 = x_prompt.reshape(t_p, d)
    xs2 = x_sample.reshape(t_s, d)

    def project(x2d, mod_, tabs, tm, rows_per_group, tiles_per_seq, idx_dtype):
        fox = _proj_fox(x2d, mod_[1], mod_[0], wqf, wkf, wvf, gqf, gkf, tm, rows_per_group)
        dsa = _proj_dsa(x2d, mod_[1], mod_[0], wqd, wkd, wvd, wqi.astype(idx_dtype), wmisc.astype(idx_dtype),
                        gqd, gkd, bf, tabs, tm, rows_per_group, tiles_per_seq)
        return fox, dsa

    (qf_p, kfb_p, vfb_p, kf_p, vf_p), (qd_p, kdb_p, vdb_p, qi_p, kia_p, kib_p, kd_p, vd_p, misc_p) = \
        project(xp2, mod_p, tabs_p, tm_p, s, s // tm_p, BF16)
    (qf_s, _, _, kf_s, vf_s), (qd_s, _, _, qi_s, _, _, kd_s, vd_s, misc_s) = \
        project(xs2, mod_s, tabs_s, tm_s, t_s, 1, F32)

    lf_p = misc_p[:, MISC_F:MISC_F + h_fox].reshape(b, s, h_fox)
    cum = _cumsum_lanes(lf_p.transpose(0, 2, 1).reshape(b * h_fox, s)).reshape(b, h_fox, s)
    o_fox_p = _fox_prompt(qf_p.reshape(b, s, nf), kfb_p.reshape(b, s, nf), vfb_p.reshape(b, s, nf), cum)
    o_dsa_p = _dsa_prompt(qi_p.reshape(b, s, ni), kia_p.reshape(b, s, LANES), kib_p.reshape(b, s, LANES),
                          misc_p.reshape(b, s, LANES), qd_p.reshape(b, s, nd),
                          kdb_p.reshape(b, s, nkv), vdb_p.reshape(b, s, nkv))

    lf_s = misc_s[:, MISC_F:MISC_F + h_fox]
    ck = cache_fox_k[layer].reshape(n_phys, PAGE_SIZE * h_fox, HEAD_DIM)
    cv = cache_fox_v[layer].reshape(n_phys, PAGE_SIZE * h_fox, HEAD_DIM)
    clf = cache_fox_logf[layer].astype(F32).reshape(n_phys, 1, PAGE_SIZE * h_fox)
    w_s = misc_s[:, MISC_W:MISC_W + H_IDX]
    n_pages = page_table.shape[1]
    o_fox_s, sc_past, sc_new = _fox_sample(
        page_table,
        qf_s.astype(F32).reshape(db, h_fox, HEAD_DIM),
        kf_s.reshape(db, h_fox, HEAD_DIM), vf_s.reshape(db, h_fox, HEAD_DIM),
        jnp.tile(lf_s, (1, PAGE_SIZE)).reshape(db, 1, PAGE_SIZE * h_fox),
        qi_s.astype(F32).reshape(db, H_IDX, D_IDX),
        jnp.broadcast_to(w_s[:, :, None], (db, H_IDX, LANES)),
        misc_s[:, MISC_KI:MISC_KI + D_IDX].reshape(db, 1, D_IDX),
        ck, cv, clf, cache_idx_k[layer].transpose(0, 2, 1))
    n_sel = min(TOPK_MAX, (past_len + dq) // 4)
    selb_past, selb_new = _select_sample(jnp.concatenate([sc_past, sc_new], axis=1), n_sel, n_pages)
    cdk = cache_dsa_k[layer].reshape(n_phys, PAGE_SIZE * KV_DSA, HEAD_DIM)
    cdv = cache_dsa_v[layer].reshape(n_phys, PAGE_SIZE * KV_DSA, HEAD_DIM)
    o_dsa_s = _dsa_sample(page_table, qd_s.astype(F32).reshape(db, h_dsa, HEAD_DIM),
                          kd_s.reshape(db, KV_DSA, HEAD_DIM), vd_s.reshape(db, KV_DSA, HEAD_DIM),
                          selb_past, selb_new, cdk, cdv)

    assert nf == nd
    wr = jnp.concatenate([w_router_group[layer],
                          w_router_expert[layer].transpose(1, 0, 2).reshape(d, N_EXPERTS),
                          jnp.zeros((d, LANES - N_GROUPS - N_EXPERTS), F32)], axis=1)
    x1_p, h2_p, route_p = _outproj(xp2, o_fox_p.reshape(t_p, nf), o_dsa_p.reshape(t_p, nd),
                                   w_out[layer].astype(BF16),
                                   mod_p[2], mod_p[4], mod_p[3], wr.astype(BF16), tm_p, s)
    x1_s, h2_s, route_s = _outproj(xs2, o_fox_s.reshape(t_s, nf), o_dsa_s.reshape(t_s, nd), w_out[layer],
                                   mod_s[2], mod_s[4], mod_s[3], wr, tm_s, t_s)

    expert_ids = jnp.concatenate([route_p[:, :2], route_s[:, :2]], axis=0).astype(I32)
    tile_expert, n_active, ends, slots = _moe_plan(expert_ids, tm_p)
    n_slots = tile_expert.shape[0] * tm_p
    x_sorted = _dispatch(ends, slots[:t_p], h2_p, slots[t_p:], h2_s, n_slots, tm_p)
    y_slots = _moe_experts(tile_expert, n_active, x_sorted, w_gate[layer], w_up[layer], w_down[layer], tm_p)
    y_p = _combine(slots[:t_p, 0], slots[:t_p, 1], x1_p, mod_p[5], route_p, y_slots, tm_p, s)
    y_s = _combine(slots[t_p:, 0], slots[t_p:, 1], x1_s, mod_s[5], route_s, y_slots, tm_s, t_s)

    def rows(kf, vf, misc, kd, vd, g, t):
        return (kf.reshape(1, g, t, h_fox, HEAD_DIM), vf.reshape(1, g, t, h_fox, HEAD_DIM),
                misc[:, MISC_F:MISC_F + h_fox].reshape(1, g, t, h_fox),
                kd.reshape(1, g, t, KV_DSA, HEAD_DIM), vd.reshape(1, g, t, KV_DSA, HEAD_DIM),
                misc[:, MISC_KI:MISC_KI + D_IDX].reshape(1, g, t, D_IDX))

    return ((y_p.reshape(b, s, d), y_s.reshape(db, dq, d))
            + rows(kf_p, vf_p, misc_p, kd_p, vd_p, b, s)
            + rows(kf_s, vf_s, misc_s, kd_s, vd_s, db, dq))
```

```python
import functools

import jax
import jax.numpy as jnp
from jax import lax
from jax.experimental import pallas as pl
from jax.experimental.pallas import tpu as pltpu

F32 = jnp.float32
BF16 = jnp.bfloat16
I32 = jnp.int32

HEAD_DIM = 128
D_IDX = 64
H_IDX = 16
KV_DSA = 2
TOPK_MAX = 256
N_GROUPS = 4
EXPERTS_PER_GROUP = 8
N_EXPERTS = N_GROUPS * EXPERTS_PER_GROUP
ROPE_THETA = 10000.0
EPS = 1e-6
PAGE_SIZE = 128

LANES = 128
NEG = -1e30
BISECT_STEPS = 36
VMEM_LIMIT = 52 * 1024 * 1024

MISC_KI = 0
MISC_F = 64
MISC_W = 72
MISC_END = 88


def _cp(sem):
    return pltpu.CompilerParams(dimension_semantics=sem, vmem_limit_bytes=VMEM_LIMIT)


def _dot(a, b):
    return jnp.dot(a, b, preferred_element_type=F32)


def _dot_nt(a, b):
    return lax.dot_general(a, b, (((1,), (1,)), ((), ())), preferred_element_type=F32)


def _adaln_kernel(c_ref, w_ref, b_ref, o_ref):
    c = c_ref[...]
    s = c / (1.0 + jnp.exp(-c))
    o_ref[...] = _dot(s.astype(BF16), w_ref[...].astype(BF16)) + b_ref[...]


def _adaln(c_all, w_ada, b_ada):
    rows, d = c_all.shape
    n = w_ada.shape[1]
    tn = 1024
    return pl.pallas_call(
        _adaln_kernel,
        grid=(n // tn,),
        in_specs=[pl.BlockSpec((rows, d), lambda j: (0, 0)),
                  pl.BlockSpec((d, tn), lambda j: (0, j)),
                  pl.BlockSpec((1, tn), lambda j: (0, j))],
        out_specs=pl.BlockSpec((rows, tn), lambda j: (0, j)),
        out_shape=jax.ShapeDtypeStruct((rows, n), F32),
        compiler_params=_cp(("parallel",)),
        name="adaln",
    )(c_all, w_ada, b_ada.reshape(1, n))


def _modulated(x, sc_ref, sh_ref):
    ms = jnp.mean(x * x, axis=-1, keepdims=True)
    return (x * lax.rsqrt(ms + EPS)) * (1.0 + sc_ref[...]) + sh_ref[...]


def _rms_heads(acc, gain, n_heads):
    outs = []
    for hh in range(n_heads):
        blk = acc[:, hh * HEAD_DIM:(hh + 1) * HEAD_DIM]
        ms = jnp.mean(blk * blk, axis=-1, keepdims=True)
        outs.append(blk * lax.rsqrt(ms + EPS) * gain)
    return outs


def _rope128(y, cos, sin_signed):
    return y * cos + pltpu.roll(y, HEAD_DIM // 2, axis=1) * sin_signed


def _rope64(y, cos, sin_signed, first_half):
    swapped = jnp.where(first_half, pltpu.roll(y, LANES - D_IDX // 2, axis=1),
                        pltpu.roll(y, D_IDX // 2, axis=1))
    return y * cos + swapped * sin_signed


def _proj_fox_kernel(x_ref, sc_ref, sh_ref, wq_ref, wk_ref, wv_ref, gq_ref, gk_ref,
                     qb_ref, kb_ref, vb_ref, k_ref, v_ref):
    h = _modulated(x_ref[...], sc_ref, sh_ref).astype(BF16)
    q = _rms_heads(_dot(h, wq_ref[...]), gq_ref[...], wq_ref.shape[1] // HEAD_DIM)
    for hh, blk in enumerate(q):
        qb_ref[:, hh * HEAD_DIM:(hh + 1) * HEAD_DIM] = blk.astype(BF16)
    k = _rms_heads(_dot(h, wk_ref[...]), gk_ref[...], wk_ref.shape[1] // HEAD_DIM)
    for hh, blk in enumerate(k):
        sl = slice(hh * HEAD_DIM, (hh + 1) * HEAD_DIM)
        k_ref[:, sl] = blk
        kb_ref[:, sl] = blk.astype(BF16)
    v = _dot(h, wv_ref[...])
    v_ref[...] = v
    vb_ref[...] = v.astype(BF16)


def _proj_dsa_kernel(x_ref, sc_ref, sh_ref, wq_ref, wk_ref, wv_ref, wi_ref, wm_ref,
                     gq_ref, gk_ref, bf_ref, c128_ref, s128_ref, c64_ref, s64_ref,
                     qb_ref, kb_ref, vb_ref, qi_ref, kia_ref, kib_ref, k_ref, v_ref, misc_ref):
    hf = _modulated(x_ref[...], sc_ref, sh_ref)
    h = hf.astype(BF16)

    def idx_dot(w_ref):
        if w_ref.dtype == F32:
            return jnp.dot(hf, w_ref[...], precision=lax.Precision.HIGHEST, preferred_element_type=F32)
        return _dot(h, w_ref[...])

    c128, s128 = c128_ref[...], s128_ref[...]
    c64, s64 = c64_ref[...], s64_ref[...]
    lane = lax.broadcasted_iota(I32, c64.shape, 1)
    first_half = (lane % D_IDX) < (D_IDX // 2)

    q = _rms_heads(_dot(h, wq_ref[...]), gq_ref[...], wq_ref.shape[1] // HEAD_DIM)
    for hh, blk in enumerate(q):
        qb_ref[:, hh * HEAD_DIM:(hh + 1) * HEAD_DIM] = _rope128(blk, c128, s128).astype(BF16)
    k = _rms_heads(_dot(h, wk_ref[...]), gk_ref[...], wk_ref.shape[1] // HEAD_DIM)
    for hh, blk in enumerate(k):
        sl = slice(hh * HEAD_DIM, (hh + 1) * HEAD_DIM)
        r = _rope128(blk, c128, s128)
        k_ref[:, sl] = r
        kb_ref[:, sl] = r.astype(BF16)
    v = _dot(h, wv_ref[...])
    v_ref[...] = v
    vb_ref[...] = v.astype(BF16)

    qi = idx_dot(wi_ref)
    for p in range(wi_ref.shape[1] // LANES):
        sl = slice(p * LANES, (p + 1) * LANES)
        qi_ref[:, sl] = _rope64(qi[:, sl], c64, s64, first_half).astype(qi_ref.dtype)

    m = idx_dot(wm_ref)
    roped = _rope64(m, c64, s64, first_half)
    xf = m + bf_ref[...]
    logsig = jnp.minimum(xf, 0.0) - jnp.log1p(jnp.exp(-jnp.abs(xf)))
    wsc = m * (H_IDX ** -0.5)
    ki_only = jnp.where(lane < MISC_F, roped, 0.0)
    misc_ref[...] = jnp.where(lane < MISC_F, roped,
                              jnp.where(lane < MISC_W, logsig,
                                        jnp.where(lane < MISC_END, wsc, 0.0)))
    kia_ref[...] = ki_only.astype(BF16)
    kib_ref[...] = pltpu.roll(ki_only, D_IDX, axis=1).astype(BF16)


def _row_specs(tm, d, rows_per_group, mod_rows):
    tiles_per_group = rows_per_group // tm
    x_spec = pl.BlockSpec((tm, d), lambda i: (i, 0))
    mod_spec = pl.BlockSpec((None, mod_rows, d), lambda i: (i // tiles_per_group, 0, 0))
    return x_spec, mod_spec


def _full(shape):
    nd = len(shape)
    return pl.BlockSpec(shape, lambda i: (0,) * nd)


def _proj_fox(x2d, sc, sh, wq, wk, wv, gq, gk, tm, rows_per_group):
    m, d = x2d.shape
    x_spec, mod_spec = _row_specs(tm, d, rows_per_group, sc.shape[1])
    n = wq.shape[1]
    o_spec = pl.BlockSpec((tm, n), lambda i: (i, 0))
    return pl.pallas_call(
        _proj_fox_kernel,
        grid=(m // tm,),
        in_specs=[x_spec, mod_spec, mod_spec, _full(wq.shape), _full(wk.shape), _full(wv.shape),
                  _full(gq.shape), _full(gk.shape)],
        out_specs=[o_spec] * 5,
        out_shape=[jax.ShapeDtypeStruct((m, n), BF16)] * 3 + [jax.ShapeDtypeStruct((m, n), F32)] * 2,
        compiler_params=_cp(("parallel",)),
        name="proj_fox",
    )(x2d, sc, sh, wq, wk, wv, gq, gk)


def _proj_dsa(x2d, sc, sh, wq, wk, wv, wi, wm, gq, gk, bf, tabs, tm, rows_per_group, tiles_per_seq):
    m, d = x2d.shape
    x_spec, mod_spec = _row_specs(tm, d, rows_per_group, sc.shape[1])
    tab_spec = pl.BlockSpec((tm, LANES), lambda i: (i % tiles_per_seq, 0))
    nq, nk, ni = wq.shape[1], wk.shape[1], wi.shape[1]

    def ospec(n):
        return pl.BlockSpec((tm, n), lambda i: (i, 0))

    def oshape(n, dt):
        return jax.ShapeDtypeStruct((m, n), dt)

    return pl.pallas_call(
        _proj_dsa_kernel,
        grid=(m // tm,),
        in_specs=[x_spec, mod_spec, mod_spec, _full(wq.shape), _full(wk.shape), _full(wv.shape),
                  _full(wi.shape), _full(wm.shape), _full(gq.shape), _full(gk.shape), _full(bf.shape),
                  tab_spec, tab_spec, tab_spec, tab_spec],
        out_specs=[ospec(nq), ospec(nk), ospec(nk), ospec(ni), ospec(LANES), ospec(LANES),
                   ospec(nk), ospec(nk), ospec(LANES)],
        out_shape=[oshape(nq, BF16), oshape(nk, BF16), oshape(nk, BF16), oshape(ni, wi.dtype),
                   oshape(LANES, BF16), oshape(LANES, BF16),
                   oshape(nk, F32), oshape(nk, F32), oshape(LANES, F32)],
        compiler_params=_cp(("parallel",)),
        name="proj_dsa",
    )(x2d, sc, sh, wq, wk, wv, wi, wm, gq, gk, bf, *tabs)


def _cumsum_kernel(x_ref, o_ref):
    x = x_ref[...]
    n = x.shape[1]
    lane = lax.broadcasted_iota(I32, x.shape, 1)
    k = 1
    while k < n:
        x = x + jnp.where(lane >= k, pltpu.roll(x, k, axis=1), 0.0)
        k *= 2
    o_ref[...] = x


def _cumsum_lanes(x):
    return pl.pallas_call(
        _cumsum_kernel,
        out_shape=jax.ShapeDtypeStruct(x.shape, F32),
        name="logf_cumsum",
    )(x)


def _fox_flash_kernel(q_ref, k_ref, v_ref, ck_ref, o_ref, s_ref, *, t, n_q):
    qi = pl.program_id(2)
    q = (q_ref[...].astype(F32) * (HEAD_DIM ** -0.5)).astype(BF16)

    def lane_fold(x, op, init):
        for j in range(x.shape[1] // LANES):
            init = op(init, x[:, j * LANES:(j + 1) * LANES])
        return init

    def attend(n_tiles):
        mx = jnp.full((t, LANES), -jnp.inf, F32)
        for c in range(n_tiles):
            s = _dot_nt(q, k_ref[c * t:(c + 1) * t, :]) - ck_ref[c]
            if c == n_tiles - 1:
                s = jnp.where(lax.broadcasted_iota(I32, s.shape, 1) <= lax.broadcasted_iota(I32, s.shape, 0),
                              s, -jnp.inf)
            s_ref[:, c * t:(c + 1) * t] = s
            mx = lane_fold(s, jnp.maximum, mx)
        m = jnp.max(mx, axis=-1, keepdims=True)
        ls = jnp.zeros((t, LANES), F32)
        acc = jnp.zeros((t, HEAD_DIM), F32)
        for c in range(n_tiles):
            p = jnp.exp(s_ref[:, c * t:(c + 1) * t] - m)
            ls = lane_fold(p, jnp.add, ls)
            acc = acc + _dot(p.astype(BF16), v_ref[c * t:(c + 1) * t, :])
        o_ref[...] = (acc / jnp.sum(ls, axis=-1, keepdims=True)).astype(o_ref.dtype)

    for n_tiles in range(1, n_q + 1):
        pl.when(qi == n_tiles - 1)(functools.partial(attend, n_tiles))


def _fox_prompt(qb, kb, vb, cum, t=512):
    b, s, hd = qb.shape
    h = hd // HEAD_DIM
    ck = cum.reshape(b, h, s // t, 1, t)
    kv_spec = pl.BlockSpec((None, s, HEAD_DIM), lambda bi, hi, qi: (bi, 0, hi))
    q_spec = pl.BlockSpec((None, t, HEAD_DIM), lambda bi, hi, qi: (bi, qi, hi))
    return pl.pallas_call(
        functools.partial(_fox_flash_kernel, t=t, n_q=s // t),
        grid=(b, h, s // t),
        in_specs=[q_spec, kv_spec, kv_spec,
                  pl.BlockSpec((None, None, s // t, 1, t), lambda bi, hi, qi: (bi, hi, 0, 0, 0))],
        out_specs=q_spec,
        out_shape=jax.ShapeDtypeStruct((b, s, hd), BF16),
        scratch_shapes=[pltpu.VMEM((t, s), F32)],
        compiler_params=_cp(("parallel", "parallel", "arbitrary")),
        name="fox_prompt",
    )(qb, kb, vb, ck)


def _order_key(x):
    bits = pltpu.bitcast(x + 0.0, I32)
    return jnp.where(bits < 0, bits ^ jnp.int32(0x7FFFFFFF), bits)


def _kth_largest_key(key, k, reduce_axes):
    shape = list(key.shape)
    for ax in reduce_axes:
        shape[ax] = 1
    sign = jnp.int32(-2 ** 31)

    def body(i, t):
        bit = lax.shift_left(jnp.int32(1), jnp.int32(31) - i)
        cand = t | bit
        ge = jnp.where(key >= (cand ^ sign), 1.0, 0.0)
        cnt = jnp.sum(ge, axis=reduce_axes, keepdims=True)
        return jnp.where(cnt >= float(k), cand, t)

    t = lax.fori_loop(0, 32, body, jnp.zeros(shape, I32))
    return t ^ sign


def _kth_largest_value(x, k, reduce_axes):
    kf = float(k)

    def count_ge(v):
        return jnp.sum(jnp.where(x >= v, 1.0, 0.0), axis=reduce_axes, keepdims=True)

    present = x > -jnp.inf
    top = jnp.max(x, axis=reduce_axes, keepdims=True)
    lo0 = jnp.min(jnp.where(present, x, jnp.inf), axis=reduce_axes, keepdims=True)

    def body(i, carry):
        lo, hi = carry
        mid = lo + 0.5 * (hi - lo)
        ge = count_ge(mid) >= kf
        return jnp.where(ge, mid, lo), jnp.where(ge, hi, mid)

    _, hi = lax.fori_loop(0, BISECT_STEPS, body, (lo0, top))
    below = jnp.max(jnp.where(x < hi, x, -jnp.inf), axis=reduce_axes, keepdims=True)
    thr = jnp.where(count_ge(top) >= kf, top, below)
    n_present = jnp.sum(jnp.where(present, 1.0, 0.0), axis=reduce_axes, keepdims=True)
    return jnp.where(n_present < kf, -jnp.inf, thr)


def _dsa_prompt_kernel(qi_ref, kia_ref, kib_ref, misc_ref, qd_ref, kd_ref, vd_ref, o_ref,
                       score_ref, selb_ref, *, tq, n_sel, q0):
    s_len = kia_ref.shape[0]
    row = (pl.program_id(1) + q0) * tq + lax.broadcasted_iota(I32, (tq, s_len), 0)
    col = lax.broadcasted_iota(I32, (tq, s_len), 1)
    causal = col <= row

    misc = misc_ref[...]
    kia, kib = kia_ref[...], kib_ref[...]
    acc = jnp.zeros((tq, s_len), F32)
    for p in range(qi_ref.shape[1] // LANES):
        qblk = qi_ref[:, p * LANES:(p + 1) * LANES]
        for half, kmat in enumerate((kia, kib)):
            hh = 2 * p + half
            w = misc[:, MISC_W + hh:MISC_W + hh + 1] * (D_IDX ** -0.5)
            acc = acc + w * jnp.maximum(_dot_nt(qblk, kmat), 0.0)
    score_ref[...] = jnp.where(causal, acc, -jnp.inf)

    def count(mask):
        return jnp.sum(jnp.where(mask, 1.0, 0.0), axis=1, keepdims=True)

    def select_ranked(vals, thr):
        need = float(n_sel) - count(vals > thr)
        ch = 256
        r_i = lax.broadcasted_iota(I32, (ch, ch), 0)
        c_i = lax.broadcasted_iota(I32, (ch, ch), 1)
        tri = jnp.where(r_i < c_i, 1.0, 0.0).astype(BF16)
        base = jnp.zeros((tq, 1), F32)
        for c in range(s_len // ch):
            sl = slice(c * ch, (c + 1) * ch)
            eq_c = jnp.where(vals[:, sl] == thr, 1.0, 0.0)
            rank = base + _dot(eq_c.astype(BF16), tri)
            take = (vals[:, sl] > thr) | ((eq_c > 0.0) & (rank < need))
            selb_ref[:, sl] = jnp.where(take & causal[:, sl], 0.0, -jnp.inf)
            base = base + jnp.sum(eq_c, axis=1, keepdims=True)

    score = score_ref[...]
    thr = _kth_largest_value(score, n_sel, (1,))
    cnt_ge = count(score >= thr)
    selb_ref[...] = jnp.where((score >= thr) & causal, 0.0, -jnp.inf)
    finite_thr = thr > -jnp.inf
    tie_rows = jnp.where((cnt_ge > float(n_sel)) & finite_thr, 1.0, 0.0)
    unresolved = jnp.where((cnt_ge < float(n_sel)) & finite_thr, 1.0, 0.0)

    @pl.when(jnp.max(tie_rows) > 0.0)
    def _():
        select_ranked(score, thr)

    @pl.when(jnp.max(unresolved) > 0.0)
    def _():
        key = _order_key(score)
        select_ranked(key, _kth_largest_key(key, n_sel, (1,)))

    n_heads = qd_ref.shape[1] // HEAD_DIM
    rep = n_heads // KV_DSA
    for hh in range(n_heads):
        g = hh // rep
        q = qd_ref[:, hh * HEAD_DIM:(hh + 1) * HEAD_DIM]
        kg = kd_ref[:, g * HEAD_DIM:(g + 1) * HEAD_DIM]
        vg = vd_ref[:, g * HEAD_DIM:(g + 1) * HEAD_DIM]
        s = _dot_nt(q, kg) * (HEAD_DIM ** -0.5) + selb_ref[...]
        m = jnp.max(s, axis=-1, keepdims=True)
        p = jnp.exp(s - m)
        l = jnp.sum(p, axis=-1, keepdims=True)
        o_ref[:, hh * HEAD_DIM:(hh + 1) * HEAD_DIM] = (_dot(p.astype(BF16), vg) / l).astype(o_ref.dtype)


def _dsa_prompt(qi, kia, kib, misc, qd, kd, vd, tq=256, tiles_per_call=1):
    b, s, _ = qi.shape
    n_sel = min(TOPK_MAX, s // 4)
    outs = []
    for q0 in range(0, s // tq, tiles_per_call):
        s_eff = (q0 + tiles_per_call) * tq

        def qspec(n, q0=q0):
            return pl.BlockSpec((None, tq, n), lambda bi, i: (bi, i + q0, 0))

        def kspec(n, s_eff=s_eff):
            return pl.BlockSpec((None, s_eff, n), lambda bi, i: (bi, 0, 0))

        outs.append(pl.pallas_call(
            functools.partial(_dsa_prompt_kernel, tq=tq, n_sel=n_sel, q0=q0),
            grid=(b, tiles_per_call),
            in_specs=[qspec(qi.shape[2]), kspec(LANES), kspec(LANES), qspec(LANES),
                      qspec(qd.shape[2]), kspec(kd.shape[2]), kspec(vd.shape[2])],
            out_specs=pl.BlockSpec((None, tq, qd.shape[2]), lambda bi, i: (bi, i, 0)),
            out_shape=jax.ShapeDtypeStruct((b, tiles_per_call * tq, qd.shape[2]), BF16),
            scratch_shapes=[pltpu.VMEM((tq, s_eff), F32), pltpu.VMEM((tq, s_eff), F32)],
            compiler_params=_cp(("parallel", "parallel")),
            name="dsa_prompt",
        )(qi, kia, kib, misc, qd, kd, vd))
    return jnp.concatenate(outs, axis=1)


def _idx_score_rows(qi, w, kt_refs):
    rows = []
    for kt_ref in kt_refs:
        d = jnp.dot(qi, kt_ref[...], precision=lax.Precision.HIGHEST, preferred_element_type=F32)
        rows.append(jnp.sum(w * jnp.maximum(d, 0.0), axis=0, keepdims=True))
    return jnp.concatenate(rows, axis=0)


def _fox_sample_kernel(pt_ref, q_ref, kn_ref, vn_ref, lfn_ref, qi_ref, wi_ref, kin_ref, *refs, pp, n_heads):
    k_refs = refs[:pp]
    v_refs = refs[pp:2 * pp]
    lf_refs = refs[2 * pp:3 * pp]
    kt_refs = refs[3 * pp:4 * pp]
    o_ref, sc_ref, scn_ref, m_ref, l_ref, acc_ref, carry_ref = refs[4 * pp:]
    c = pl.program_id(1)
    width = PAGE_SIZE * n_heads
    q = q_ref[...] * (HEAD_DIM ** -0.5)
    qb = q.astype(BF16)
    qi = qi_ref[...]
    wi = wi_ref[...] * (D_IDX ** -0.5)
    own_head = (lax.broadcasted_iota(I32, (n_heads, width), 1) % n_heads
                == lax.broadcasted_iota(I32, (n_heads, width), 0))

    @pl.when(c == 0)
    def _():
        s_new = jnp.sum(q * kn_ref[...], axis=-1, keepdims=True)
        m_ref[...] = jnp.broadcast_to(s_new, m_ref.shape)
        l_ref[...] = jnp.ones(l_ref.shape, F32)
        acc_ref[...] = vn_ref[...]
        carry_ref[...] = lfn_ref[...]
        d = jnp.sum(qi * kin_ref[...], axis=-1, keepdims=True)
        sn = jnp.sum(wi * jnp.maximum(d, 0.0), axis=0, keepdims=True)
        r = lax.broadcasted_iota(I32, scn_ref.shape, 0)
        ln = lax.broadcasted_iota(I32, scn_ref.shape, 1)
        scn_ref[...] = jnp.where((r == 0) & (ln == 0), jnp.broadcast_to(sn, scn_ref.shape), -jnp.inf)

    sc_ref[...] = _idx_score_rows(qi, wi, kt_refs)

    lf = jnp.concatenate([r[...] for r in lf_refs], axis=0)
    lane = lax.broadcasted_iota(I32, lf.shape, 1)
    tot, suf = lf, lf
    k = n_heads
    while k < width:
        tot = tot + pltpu.roll(tot, k, axis=1)
        suf = suf + jnp.where(lane + k < width, pltpu.roll(suf, width - k, axis=1), 0.0)
        k *= 2
    later = suf - lf
    run = carry_ref[...]
    bias = [None] * pp
    for j in reversed(range(pp)):
        bias[j] = run + later[j:j + 1, :]
        run = run + tot[j:j + 1, :]
    carry_ref[...] = run

    logits = []
    for j in range(pp):
        s = _dot_nt(qb, k_refs[j][...].astype(BF16))
        logits.append(jnp.where(own_head, s + bias[j], NEG))
    m_prev = m_ref[...]
    m_new = m_prev
    for j in range(pp):
        m_new = jnp.maximum(m_new, jnp.max(logits[j], axis=-1, keepdims=True))
    alpha = jnp.exp(m_prev - m_new)
    l_new = alpha * l_ref[...]
    acc = alpha * acc_ref[...]
    for j in range(pp):
        p = jnp.exp(logits[j] - m_new[:, :1])
        l_new = l_new + jnp.sum(p, axis=-1, keepdims=True)
        acc = acc + _dot(p.astype(BF16), v_refs[j][...].astype(BF16))
    m_ref[...] = m_new
    l_ref[...] = l_new
    acc_ref[...] = acc

    @pl.when(c == pl.num_programs(1) - 1)
    def _():
        o_ref[...] = acc / l_new


def _fox_sample(page_table, q, k_new, v_new, lf_new, qi, wi, ki_new, cache_k, cache_v, cache_lf, cache_idx_t, pp=8):
    db, n_heads, _ = q.shape
    n_pages = page_table.shape[1]
    nc = n_pages // pp
    width = PAGE_SIZE * n_heads

    def page_idx(j):
        return lambda bi, ci, pt: (pt[bi, (nc - 1 - ci) * pp + j], 0, 0)

    def per_seq(shape):
        return pl.BlockSpec((None,) + shape, lambda bi, ci, pt: (bi, 0, 0))

    vec_spec = per_seq((n_heads, HEAD_DIM))
    kv_specs = [pl.BlockSpec((None, width, HEAD_DIM), page_idx(j)) for j in range(pp)]
    lf_specs = [pl.BlockSpec((None, 1, width), page_idx(j)) for j in range(pp)]
    kt_specs = [pl.BlockSpec((None, D_IDX, PAGE_SIZE), page_idx(j)) for j in range(pp)]
    grid_spec = pltpu.PrefetchScalarGridSpec(
        num_scalar_prefetch=1,
        grid=(db, nc),
        in_specs=[vec_spec, vec_spec, vec_spec, per_seq((1, width)),
                  per_seq((H_IDX, D_IDX)), per_seq((H_IDX, LANES)), per_seq((1, D_IDX))]
                 + kv_specs + kv_specs + lf_specs + kt_specs,
        out_specs=[vec_spec,
                   pl.BlockSpec((None, pp, PAGE_SIZE), lambda bi, ci, pt: (bi, nc - 1 - ci, 0)),
                   per_seq((8, LANES))],
        scratch_shapes=[pltpu.VMEM((n_heads, HEAD_DIM), F32)] * 3 + [pltpu.VMEM((1, width), F32)],
    )
    return pl.pallas_call(
        functools.partial(_fox_sample_kernel, pp=pp, n_heads=n_heads),
        grid_spec=grid_spec,
        out_shape=[jax.ShapeDtypeStruct((db, n_heads, HEAD_DIM), F32),
                   jax.ShapeDtypeStruct((db, n_pages, PAGE_SIZE), F32),
                   jax.ShapeDtypeStruct((db, 8, LANES), F32)],
        compiler_params=_cp(("parallel", "arbitrary")),
        name="fox_sample",
    )(page_table, q, k_new, v_new, lf_new, qi, wi, ki_new,
      *([cache_k] * pp), *([cache_v] * pp), *([cache_lf] * pp), *([cache_idx_t] * pp))


def _select_kernel(s_ref, past_ref, new_ref, take_ref, *, n_sel, n_pages):
    score = s_ref[...]
    db, rows, _ = score.shape
    valid = score > -jnp.inf

    def count(mask):
        return jnp.sum(jnp.where(mask, 1.0, 0.0), axis=(1, 2), keepdims=True)

    def select_ranked(vals, thr):
        need = float(n_sel) - count(vals > thr)
        eq = jnp.where(vals == thr, 1.0, 0.0)
        r_i = lax.broadcasted_iota(I32, (LANES, LANES), 0)
        c_i = lax.broadcasted_iota(I32, (LANES, LANES), 1)
        tri = jnp.where(r_i < c_i, 1.0, 0.0).astype(BF16)
        within = _dot(eq.reshape(db * rows, LANES).astype(BF16), tri).reshape(db, rows, LANES)
        row_cnt = jnp.broadcast_to(jnp.sum(eq, axis=2, keepdims=True), score.shape)
        rows_pad = 2 * LANES
        cnt_pad = jnp.concatenate([row_cnt, jnp.zeros((db, rows_pad - rows, LANES), F32)], axis=1)
        rr = lax.broadcasted_iota(I32, (db, rows_pad, rows_pad), 1)
        rc = lax.broadcasted_iota(I32, (db, rows_pad, rows_pad), 2)
        before = jnp.einsum("bij,bjl->bil", jnp.where(rc < rr, 1.0, 0.0).astype(BF16), cnt_pad.astype(BF16),
                            preferred_element_type=F32)[:, :rows, :]
        take = (vals > thr) | ((eq > 0.0) & (within + before < need))
        take_ref[...] = jnp.where(take & valid, 1.0, 0.0)

    thr = _kth_largest_value(score, n_sel, (1, 2))
    cnt_ge = count(score >= thr)
    take_ref[...] = jnp.where((score >= thr) & valid, 1.0, 0.0)
    finite_thr = thr > -jnp.inf
    ties = jnp.where((cnt_ge > float(n_sel)) & finite_thr, 1.0, 0.0)
    unresolved = jnp.where((cnt_ge < float(n_sel)) & finite_thr, 1.0, 0.0)

    @pl.when(jnp.max(ties) > 0.0)
    def _():
        select_ranked(score, thr)

    @pl.when(jnp.max(unresolved) > 0.0)
    def _():
        key = _order_key(score)
        select_ranked(key, _kth_largest_key(key, n_sel, (1, 2)))

    take01 = take_ref[...]
    new_ref[...] = jnp.where(take01[:, n_pages:, :] > 0.5, 0.0, NEG)
    t_i = lax.broadcasted_iota(I32, (LANES, LANES * KV_DSA), 0)
    l_i = lax.broadcasted_iota(I32, (LANES, LANES * KV_DSA), 1)
    spread = jnp.where(l_i // KV_DSA == t_i, 1.0, 0.0).astype(BF16)
    past = _dot(take01[:, :n_pages, :].reshape(db * n_pages, LANES).astype(BF16), spread)
    past_ref[...] = jnp.where(past > 0.5, 0.0, NEG).reshape(db, n_pages, LANES * KV_DSA)


def _select_sample(scores, n_sel, n_pages):
    db, rows, lanes = scores.shape
    return pl.pallas_call(
        functools.partial(_select_kernel, n_sel=n_sel, n_pages=n_pages),
        out_shape=[jax.ShapeDtypeStruct((db, n_pages, lanes * KV_DSA), F32),
                   jax.ShapeDtypeStruct((db, rows - n_pages, lanes), F32)],
        scratch_shapes=[pltpu.VMEM((db, rows, lanes), F32)],
        compiler_params=pltpu.CompilerParams(vmem_limit_bytes=VMEM_LIMIT),
        name="select_sample",
    )(scores)


def _dsa_sample_kernel(pt_ref, q_ref, kn_ref, vn_ref, selb_ref, selbn_ref, *refs, pp, n_heads):
    k_refs = refs[:pp]
    v_refs = refs[pp:2 * pp]
    o_ref, m_ref, l_ref, acc_ref = refs[2 * pp:]
    c = pl.program_id(1)
    rep = n_heads // KV_DSA
    width = PAGE_SIZE * KV_DSA
    q = q_ref[...] * (HEAD_DIM ** -0.5)
    qb = q.astype(BF16)
    grp = lax.broadcasted_iota(I32, (n_heads, HEAD_DIM), 0) // rep
    own_kv = (lax.broadcasted_iota(I32, (n_heads, width), 1) % KV_DSA
              == lax.broadcasted_iota(I32, (n_heads, width), 0) // rep)

    @pl.when(c == 0)
    def _():
        kn = kn_ref[...]
        vn = vn_ref[...]
        s_new = jnp.zeros((n_heads, 1), F32)
        v_rows = jnp.zeros((n_heads, HEAD_DIM), F32)
        for g in range(KV_DSA):
            sg = jnp.sum(q * kn[g:g + 1, :], axis=-1, keepdims=True)
            s_new = jnp.where(grp[:, :1] == g, sg, s_new)
            v_rows = jnp.where(grp == g, jnp.broadcast_to(vn[g:g + 1, :], v_rows.shape), v_rows)
        s_new = s_new + selbn_ref[0:1, 0:1]
        m_ref[...] = jnp.broadcast_to(jnp.maximum(s_new, NEG), m_ref.shape)
        p_new = jnp.exp(s_new - m_ref[:, :1])
        l_ref[...] = jnp.broadcast_to(p_new, l_ref.shape)
        acc_ref[...] = p_new * v_rows

    selb = selb_ref[...]
    logits = []
    for j in range(pp):
        s = _dot_nt(qb, k_refs[j][...].astype(BF16))
        logits.append(jnp.where(own_kv, s + selb[j:j + 1, :], NEG))

    m_prev = m_ref[...]
    m_new = m_prev
    for j in range(pp):
        m_new = jnp.maximum(m_new, jnp.max(logits[j], axis=-1, keepdims=True))
    alpha = jnp.exp(m_prev - m_new)
    l_new = alpha * l_ref[...]
    acc = alpha * acc_ref[...]
    for j in range(pp):
        p = jnp.exp(logits[j] - m_new[:, :1])
        l_new = l_new + jnp.sum(p, axis=-1, keepdims=True)
        acc = acc + _dot(p.astype(BF16), v_refs[j][...].astype(BF16))
    m_ref[...] = m_new
    l_ref[...] = l_new
    acc_ref[...] = acc

    @pl.when(c == pl.num_programs(1) - 1)
    def _():
        o_ref[...] = acc / l_new


def _dsa_sample(page_table, q, k_new, v_new, selb_past, selb_new, cache_k, cache_v, pp=16):
    db, n_heads, _ = q.shape
    n_pages = page_table.shape[1]
    nc = n_pages // pp
    width = PAGE_SIZE * KV_DSA

    def page_idx(j):
        return lambda bi, ci, pt: (pt[bi, ci * pp + j], 0, 0)

    q_spec = pl.BlockSpec((None, n_heads, HEAD_DIM), lambda bi, ci, pt: (bi, 0, 0))
    n_spec = pl.BlockSpec((None, KV_DSA, HEAD_DIM), lambda bi, ci, pt: (bi, 0, 0))
    kv_specs = [pl.BlockSpec((None, width, HEAD_DIM), page_idx(j)) for j in range(pp)]
    grid_spec = pltpu.PrefetchScalarGridSpec(
        num_scalar_prefetch=1,
        grid=(db, nc),
        in_specs=[q_spec, n_spec, n_spec,
                  pl.BlockSpec((None, pp, width), lambda bi, ci, pt: (bi, ci, 0)),
                  pl.BlockSpec((None,) + selb_new.shape[1:], lambda bi, ci, pt: (bi, 0, 0))]
                 + kv_specs + kv_specs,
        out_specs=q_spec,
        scratch_shapes=[pltpu.VMEM((n_heads, HEAD_DIM), F32)] * 3,
    )
    return pl.pallas_call(
        functools.partial(_dsa_sample_kernel, pp=pp, n_heads=n_heads),
        grid_spec=grid_spec,
        out_shape=jax.ShapeDtypeStruct((db, n_heads, HEAD_DIM), F32),
        compiler_params=_cp(("parallel", "arbitrary")),
        name="dsa_sample",
    )(page_table, q, k_new, v_new, selb_past, selb_new, *([cache_k] * pp), *([cache_v] * pp))


def _outproj_kernel(x_ref, of_ref, od_ref, wf_ref, wd_ref, ga_ref, sc_ref, sh_ref, wr_ref,
                    x1_ref, h2_ref, route_ref):
    def proj(a_ref, w_ref):
        if w_ref.dtype == F32:
            return jnp.dot(a_ref[...].astype(F32), w_ref[...], precision=lax.Precision.HIGHEST,
                           preferred_element_type=F32)
        return _dot(a_ref[...].astype(BF16), w_ref[...])

    mix = proj(of_ref, wf_ref) + proj(od_ref, wd_ref)
    x1 = x_ref[...] + ga_ref[...] * mix
    x1_ref[...] = x1
    h2 = _modulated(x1, sc_ref, sh_ref)
    h2_ref[...] = h2
    if wr_ref.dtype == F32:
        logits = jnp.dot(h2, wr_ref[...], precision=lax.Precision.HIGHEST, preferred_element_type=F32)
    else:
        logits = _dot(h2.astype(BF16), wr_ref[...])
    lane = lax.broadcasted_iota(I32, logits.shape, 1).astype(F32)

    def first_argmax(vals, mask):
        v = jnp.where(mask, vals, -jnp.inf)
        mx = jnp.max(v, axis=-1, keepdims=True)
        idx = jnp.min(jnp.where(v == mx, lane, float(LANES)), axis=-1, keepdims=True)
        return mx, idx

    is_g = lane < float(N_GROUPS)
    mg, g_sel = first_argmax(logits, is_g)
    p_gsel = 1.0 / jnp.sum(jnp.where(is_g, jnp.exp(logits - mg), 0.0), axis=-1, keepdims=True)
    lo = float(N_GROUPS) + g_sel * float(EXPERTS_PER_GROUP)
    in_grp = (lane >= lo) & (lane < lo + float(EXPERTS_PER_GROUP))
    m1, i1 = first_argmax(logits, in_grp)
    m2, i2 = first_argmax(logits, in_grp & (lane != i1))
    e2 = jnp.exp(m2 - m1)
    w1 = p_gsel / (1.0 + e2)
    w2 = p_gsel * e2 / (1.0 + e2)
    route_ref[...] = jnp.where(lane == 0.0, i1 - float(N_GROUPS),
                               jnp.where(lane == 1.0, i2 - float(N_GROUPS),
                                         jnp.where(lane == 2.0, w1, jnp.where(lane == 3.0, w2, 0.0))))


def _outproj(x2d, o_fox, o_dsa, w_out, ga, sc, sh, wr, tm, rows_per_group):
    m, d = x2d.shape
    x_spec, mod_spec = _row_specs(tm, d, rows_per_group, sc.shape[1])
    half = o_fox.shape[1]
    half_spec = pl.BlockSpec((tm, half), lambda i: (i, 0))
    wf, wd = w_out, w_out
    return pl.pallas_call(
        _outproj_kernel,
        grid=(m // tm,),
        in_specs=[x_spec, half_spec, half_spec,
                  pl.BlockSpec((half, d), lambda i: (0, 0)), pl.BlockSpec((half, d), lambda i: (1, 0)),
                  mod_spec, mod_spec, mod_spec, _full(wr.shape)],
        out_specs=[x_spec, x_spec, pl.BlockSpec((tm, LANES), lambda i: (i, 0))],
        out_shape=[jax.ShapeDtypeStruct((m, d), F32), jax.ShapeDtypeStruct((m, d), F32),
                   jax.ShapeDtypeStruct((m, LANES), F32)],
        compiler_params=_cp(("parallel",)),
        name="outproj_router",
    )(x2d, o_fox, o_dsa, wf, wd, ga, sc, sh, wr)


def _dispatch_kernel(ends_ref, s0_ref, s1_ref, h_ref, t0_ref, t1_ref, ht_ref, xs_hbm, zbuf, sem, zsem,
                     *, tm, n_main):
    i = pl.program_id(0)

    @pl.when(i == 0)
    def _():
        zbuf[...] = jnp.zeros(zbuf.shape, F32)

        def fill(e):
            start = pl.multiple_of(jnp.maximum(ends_ref[0, e] - tm, 0), tm)
            return pltpu.make_async_copy(zbuf, xs_hbm.at[pl.ds(start, tm)], zsem)

        for e in range(N_EXPERTS):
            fill(e).start()
        for e in range(N_EXPERTS):
            fill(e).wait()

        def clear_unused(k, carry):
            cp = pltpu.make_async_copy(zbuf, xs_hbm.at[pl.ds(pl.multiple_of(k * tm, tm), tm)], zsem)
            cp.start()
            cp.wait()
            return carry

        lax.fori_loop(ends_ref[0, N_EXPERTS - 1] // tm, xs_hbm.shape[0] // tm, clear_unused, 0)

    def scatter(a_ref, b_ref, src_ref, rows):
        def issue(r, carry):
            src = src_ref.at[pl.ds(r, 1)]
            pltpu.make_async_copy(src, xs_hbm.at[pl.ds(a_ref[0, r], 1)], sem).start()
            pltpu.make_async_copy(src, xs_hbm.at[pl.ds(b_ref[0, r], 1)], sem).start()
            return carry

        lax.fori_loop(0, rows, issue, 0, unroll=8)
        pltpu.make_async_copy(src_ref, xs_hbm.at[pl.ds(0, rows)], sem).wait()
        pltpu.make_async_copy(src_ref, xs_hbm.at[pl.ds(0, rows)], sem).wait()

    @pl.when(i < n_main)
    def _():
        scatter(s0_ref, s1_ref, h_ref, tm)

    @pl.when(i == n_main)
    def _():
        scatter(t0_ref, t1_ref, ht_ref, ht_ref.shape[0])


def _dispatch(ends, slots_main, h_main, slots_tail, h_tail, n_slots, tm):
    m, d = h_main.shape
    mt = h_tail.shape[0]
    n_main = m // tm
    smem = pltpu.SMEM

    def main_idx(i):
        return (jnp.minimum(i, n_main - 1), 0, 0)

    id_spec = pl.BlockSpec((None, 1, tm), main_idx, memory_space=smem)
    tail_spec = pl.BlockSpec((None, 1, mt), lambda i: (0, 0, 0), memory_space=smem)
    return pl.pallas_call(
        functools.partial(_dispatch_kernel, tm=tm, n_main=n_main),
        grid=(n_main + 1,),
        in_specs=[pl.BlockSpec(memory_space=smem), id_spec, id_spec,
                  pl.BlockSpec((tm, d), lambda i: (jnp.minimum(i, n_main - 1), 0)),
                  tail_spec, tail_spec, pl.BlockSpec((mt, d), lambda i: (0, 0))],
        out_specs=pl.BlockSpec(memory_space=pl.ANY),
        out_shape=jax.ShapeDtypeStruct((n_slots, d), F32),
        scratch_shapes=[pltpu.VMEM((tm, d), F32), pltpu.SemaphoreType.DMA(()), pltpu.SemaphoreType.DMA(())],
        compiler_params=_cp(("arbitrary",)),
        name="moe_dispatch",
    )(ends.reshape(1, N_EXPERTS),
      slots_main[:, 0].reshape(n_main, 1, tm), slots_main[:, 1].reshape(n_main, 1, tm), h_main,
      slots_tail[:, 0].reshape(1, 1, mt), slots_tail[:, 1].reshape(1, 1, mt), h_tail)


def _moe_kernel(te_ref, na_ref, x_ref, wg_ref, wu_ref, wd_ref, o_ref, wg_b, wu_b, wd_b):
    t = pl.program_id(0)

    @pl.when(t < na_ref[0])
    def _():
        @pl.when((t == 0) | (te_ref[t] != te_ref[jnp.maximum(t - 1, 0)]))
        def _():
            wg_b[...] = wg_ref[...].astype(BF16)
            wu_b[...] = wu_ref[...].astype(BF16)
            wd_b[...] = wd_ref[...].astype(BF16)

        h = x_ref[...].astype(BF16)
        a = _dot(h, wg_b[...])
        u = _dot(h, wu_b[...])
        act = (a / (1.0 + jnp.exp(-a))) * u
        o_ref[...] = _dot(act.astype(BF16), wd_b[...])

    @pl.when(t >= na_ref[0])
    def _():
        o_ref[...] = jnp.zeros(o_ref.shape, F32)


def _moe_experts(tile_expert, n_active, x_sorted, w_gate, w_up, w_down, tm):
    n_tiles = tile_expert.shape[0]
    d = x_sorted.shape[1]
    f = w_gate.shape[2]

    def row_idx(t, te, na):
        return (jnp.minimum(t, na[0] - 1), 0)

    def w_idx(t, te, na):
        return (te[jnp.minimum(t, na[0] - 1)], 0, 0)

    grid_spec = pltpu.PrefetchScalarGridSpec(
        num_scalar_prefetch=2,
        grid=(n_tiles,),
        in_specs=[pl.BlockSpec((tm, d), row_idx),
                  pl.BlockSpec((None, d, f), w_idx),
                  pl.BlockSpec((None, d, f), w_idx),
                  pl.BlockSpec((None, f, d), w_idx)],
        out_specs=pl.BlockSpec((tm, d), lambda t, te, na: (t, 0)),
        scratch_shapes=[pltpu.VMEM((d, f), BF16), pltpu.VMEM((d, f), BF16), pltpu.VMEM((f, d), BF16)],
    )
    return pl.pallas_call(
        _moe_kernel,
        grid_spec=grid_spec,
        out_shape=jax.ShapeDtypeStruct((n_tiles * tm, d), F32),
        compiler_params=_cp(("arbitrary",)),
        name="moe_experts",
    )(tile_expert, n_active, x_sorted, w_gate, w_up, w_down)


def _combine_kernel(s0_ref, s1_ref, x1_ref, gm_ref, route_ref, y_hbm, o_ref, b0, b1, sem, *, tm):
    def copies(r):
        return (pltpu.make_async_copy(y_hbm.at[pl.ds(s0_ref[0, r], 1)], b0.at[pl.ds(r, 1)], sem),
                pltpu.make_async_copy(y_hbm.at[pl.ds(s1_ref[0, r], 1)], b1.at[pl.ds(r, 1)], sem))

    def issue(r, carry):
        c0, c1 = copies(r)
        c0.start()
        c1.start()
        return carry

    lax.fori_loop(0, tm, issue, 0, unroll=8)
    pltpu.make_async_copy(y_hbm.at[pl.ds(0, tm)], b0, sem).wait()
    pltpu.make_async_copy(y_hbm.at[pl.ds(0, tm)], b1, sem).wait()
    route = route_ref[...]
    moe = route[:, 2:3] * b0[...] + route[:, 3:4] * b1[...]
    o_ref[...] = x1_ref[...] + gm_ref[...] * moe


def _combine(slot0, slot1, x1, gm, route, y_slots, tm, rows_per_group):
    m, d = x1.shape
    x_spec, mod_spec = _row_specs(tm, d, rows_per_group, gm.shape[1])
    n_tiles = m // tm
    id_spec = pl.BlockSpec((None, 1, tm), lambda i: (i, 0, 0), memory_space=pltpu.SMEM)
    return pl.pallas_call(
        functools.partial(_combine_kernel, tm=tm),
        grid=(n_tiles,),
        in_specs=[id_spec, id_spec, x_spec, mod_spec,
                  pl.BlockSpec((tm, LANES), lambda i: (i, 0)),
                  pl.BlockSpec(memory_space=pl.ANY)],
        out_specs=x_spec,
        out_shape=jax.ShapeDtypeStruct((m, d), F32),
        scratch_shapes=[pltpu.VMEM((tm, d), F32), pltpu.VMEM((tm, d), F32), pltpu.SemaphoreType.DMA(())],
        compiler_params=_cp(("arbitrary",)),
        name="moe_combine",
    )(slot0.reshape(n_tiles, 1, tm), slot1.reshape(n_tiles, 1, tm), x1, gm, route, y_slots)


def _moe_plan(expert_ids, tm):
    t_all = expert_ids.shape[0]
    flat = expert_ids.reshape(-1)
    onehot = (flat[:, None] == jnp.arange(N_EXPERTS, dtype=I32)[None, :]).astype(I32)
    rank = jnp.sum((jnp.cumsum(onehot, axis=0) - onehot) * onehot, axis=1)
    counts = jnp.sum(onehot, axis=0)
    padded = ((counts + tm - 1) // tm) * tm
    ends = jnp.cumsum(padded).astype(I32)
    starts = ends - padded
    slot = jnp.sum(onehot * starts[None, :], axis=1) + rank
    n_tiles = (2 * t_all + N_EXPERTS * (tm - 1)) // tm + 1
    tile_start = jnp.arange(n_tiles, dtype=I32) * tm
    tile_expert = jnp.minimum(jnp.sum((tile_start[:, None] >= ends[None, :]).astype(I32), axis=1),
                              N_EXPERTS - 1)
    n_active = (ends[-1:] // tm).astype(I32)
    return tile_expert.astype(I32), n_active, ends, slot.reshape(t_all, 2).astype(I32)


def _rope_tables(pos, dim, lanes):
    half = dim // 2
    inv = ROPE_THETA ** (-jnp.arange(half, dtype=F32) / half)
    ang = pos.astype(F32)[:, None] * inv[None, :]
    cos, sin = jnp.cos(ang), jnp.sin(ang)
    reps = lanes // dim
    cos_t = jnp.tile(jnp.concatenate([cos, cos], axis=-1), (1, reps))
    sin_t = jnp.tile(jnp.concatenate([-sin, sin], axis=-1), (1, reps))
    return cos_t, sin_t


def kernel(x_prompt, x_sample, cache_fox_k, cache_fox_v, cache_fox_logf, cache_dsa_k, cache_dsa_v,
           cache_idx_k, page_table, c_prompt, c_sample, w_in, b_forget, q_gain_fox, k_gain_fox,
           q_gain_dsa, k_gain_dsa, w_out, w_ada, b_ada, w_router_group, w_router_expert,
           w_gate, w_up, w_down):
    b, s, d = x_prompt.shape
    db, dq, _ = x_sample.shape
    depth = w_in.shape[0]
    assert depth == 1 and dq == 1
    past_len = page_table.shape[1] * PAGE_SIZE
    n_phys = cache_fox_k.shape[1]
    h_fox = cache_fox_k.shape[3]
    h_dsa = d // (2 * HEAD_DIM)
    t_p, t_s = b * s, db * dq
    t_all = t_p + t_s
    tm_p, tm_s = 256, t_s
    layer = 0

    n_c = b + db
    pad = (-n_c) % 8
    c_all = jnp.concatenate([c_prompt, c_sample, jnp.zeros((pad, d), F32)], axis=0)
    mod = _adaln(c_all, w_ada[layer], b_ada[layer])
    mods = [mod[:, i * d:(i + 1) * d] for i in range(6)]
    mod_p = [m_[:b].reshape(b, 1, d) for m_ in mods]
    mod_s = [m_[b:b + db].reshape(1, db, d) for m_ in mods]

    w = w_in[layer]
    nf = h_fox * HEAD_DIM
    nd = h_dsa * HEAD_DIM
    nkv = KV_DSA * HEAD_DIM
    ni = H_IDX * D_IDX
    o = 0
    wqf = w[:, o:o + nf].astype(BF16); o += nf
    wkf = w[:, o:o + nf].astype(BF16); o += nf
    wvf = w[:, o:o + nf].astype(BF16); o += nf
    w_fl = w[:, o:o + h_fox]; o += h_fox
    wqd = w[:, o:o + nd].astype(BF16); o += nd
    wkd = w[:, o:o + nkv].astype(BF16); o += nkv
    wvd = w[:, o:o + nkv].astype(BF16); o += nkv
    wqi = w[:, o:o + ni]; o += ni
    w_ki = w[:, o:o + D_IDX]; o += D_IDX
    w_wi = w[:, o:o + H_IDX]; o += H_IDX
    assert h_fox <= MISC_W - MISC_F
    wmisc = jnp.concatenate([w_ki, w_fl, jnp.zeros((d, MISC_W - MISC_F - h_fox), F32), w_wi,
                             jnp.zeros((d, LANES - MISC_END), F32)], axis=1)
    bf = jnp.zeros((1, LANES), F32).at[0, MISC_F:MISC_F + h_fox].set(b_forget[layer])
    gqf, gkf = q_gain_fox[layer].reshape(1, HEAD_DIM), k_gain_fox[layer].reshape(1, HEAD_DIM)
    gqd, gkd = q_gain_dsa[layer].reshape(1, HEAD_DIM), k_gain_dsa[layer].reshape(1, HEAD_DIM)

    pos_p = jnp.arange(s, dtype=I32)
    pos_s = jnp.full((t_s,), past_len, I32)
    tabs_p = _rope_tables(pos_p, HEAD_DIM, LANES) + _rope_tables(pos_p, D_IDX, LANES)
    tabs_s = _rope_tables(pos_s, HEAD_DIM, LANES) + _rope_tables(pos_s, D_IDX, LANES)

    xp2 = x_prompt.reshape(t_p, d)
    xs2 = x_sample.reshape(t_s, d)

    def project(x2d, mod_, tabs, tm, rows_per_group, tiles_per_seq, idx_dtype):
        fox = _proj_fox(x2d, mod_[1], mod_[0], wqf, wkf, wvf, gqf, gkf, tm, rows_per_group)
        dsa = _proj_dsa(x2d, mod_[1], mod_[0], wqd, wkd, wvd, wqi.astype(idx_dtype), wmisc.astype(idx_dtype),
                        gqd, gkd, bf, tabs, tm, rows_per_group, tiles_per_seq)
        return fox, dsa

    (qf_p, kfb_p, vfb_p, kf_p, vf_p), (qd_p, kdb_p, vdb_p, qi_p, kia_p, kib_p, kd_p, vd_p, misc_p) = \
        project(xp2, mod_p, tabs_p, tm_p, s, s // tm_p, BF16)
    (qf_s, _, _, kf_s, vf_s), (qd_s, _, _, qi_s, _, _, kd_s, vd_s, misc_s) = \
        project(xs2, mod_s, tabs_s, tm_s, t_s, 1, F32)

    lf_p = misc_p[:, MISC_F:MISC_F + h_fox].reshape(b, s, h_fox)
    cum = _cumsum_lanes(lf_p.transpose(0, 2, 1).reshape(b * h_fox, s)).reshape(b, h_fox, s)
    o_fox_p = _fox_prompt(qf_p.reshape(b, s, nf), kfb_p.reshape(b, s, nf), vfb_p.reshape(b, s, nf), cum)
    o_dsa_p = _dsa_prompt(qi_p.reshape(b, s, ni), kia_p.reshape(b, s, LANES), kib_p.reshape(b, s, LANES),
                          misc_p.reshape(b, s, LANES), qd_p.reshape(b, s, nd),
                          kdb_p.reshape(b, s, nkv), vdb_p.reshape(b, s, nkv))

    lf_s = misc_s[:, MISC_F:MISC_F + h_fox]
    ck = cache_fox_k[layer].reshape(n_phys, PAGE_SIZE * h_fox, HEAD_DIM)
    cv = cache_fox_v[layer].reshape(n_phys, PAGE_SIZE * h_fox, HEAD_DIM)
    clf = cache_fox_logf[layer].astype(F32).reshape(n_phys, 1, PAGE_SIZE * h_fox)
    w_s = misc_s[:, MISC_W:MISC_W + H_IDX]
    n_pages = page_table.shape[1]
    o_fox_s, sc_past, sc_new = _fox_sample(
        page_table,
        qf_s.astype(F32).reshape(db, h_fox, HEAD_DIM),
        kf_s.reshape(db, h_fox, HEAD_DIM), vf_s.reshape(db, h_fox, HEAD_DIM),
        jnp.tile(lf_s, (1, PAGE_SIZE)).reshape(db, 1, PAGE_SIZE * h_fox),
        qi_s.astype(F32).reshape(db, H_IDX, D_IDX),
        jnp.broadcast_to(w_s[:, :, None], (db, H_IDX, LANES)),
        misc_s[:, MISC_KI:MISC_KI + D_IDX].reshape(db, 1, D_IDX),
        ck, cv, clf, cache_idx_k[layer].transpose(0, 2, 1))
    n_sel = min(TOPK_MAX, (past_len + dq) // 4)
    selb_past, selb_new = _select_sample(jnp.concatenate([sc_past, sc_new], axis=1), n_sel, n_pages)
    cdk = cache_dsa_k[layer].reshape(n_phys, PAGE_SIZE * KV_DSA, HEAD_DIM)
    cdv = cache_dsa_v[layer].reshape(n_phys, PAGE_SIZE * KV_DSA, HEAD_DIM)
    o_dsa_s = _dsa_sample(page_table, qd_s.astype(F32).reshape(db, h_dsa, HEAD_DIM),
                          kd_s.reshape(db, KV_DSA, HEAD_DIM), vd_s.reshape(db, KV_DSA, HEAD_DIM),
                          selb_past, selb_new, cdk, cdv)

    assert nf == nd
    wr = jnp.concatenate([w_router_group[layer],
                          w_router_expert[layer].transpose(1, 0, 2).reshape(d, N_EXPERTS),
                          jnp.zeros((d, LANES - N_GROUPS - N_EXPERTS), F32)], axis=1)
    x1_p, h2_p, route_p = _outproj(xp2, o_fox_p.reshape(t_p, nf), o_dsa_p.reshape(t_p, nd),
                                   w_out[layer].astype(BF16),
                                   mod_p[2], mod_p[4], mod_p[3], wr.astype(BF16), tm_p, s)
    x1_s, h2_s, route_s = _outproj(xs2, o_fox_s.reshape(t_s, nf), o_dsa_s.reshape(t_s, nd), w_out[layer],
                                   mod_s[2], mod_s[4], mod_s[3], wr, tm_s, t_s)

    expert_ids = jnp.concatenate([route_p[:, :2], route_s[:, :2]], axis=0).astype(I32)
    tile_expert, n_active, ends, slots = _moe_plan(expert_ids, tm_p)
    n_slots = tile_expert.shape[0] * tm_p
    x_sorted = _dispatch(ends, slots[:t_p], h2_p, slots[t_p:], h2_s, n_slots, tm_p)
    y_slots = _moe_experts(tile_expert, n_active, x_sorted, w_gate[layer], w_up[layer], w_down[layer], tm_p)
    y_p = _combine(slots[:t_p, 0], slots[:t_p, 1], x1_p, mod_p[5], route_p, y_slots, tm_p, s)
    y_s = _combine(slots[t_p:, 0], slots[t_p:, 1], x1_s, mod_s[5], route_s, y_slots, tm_s, t_s)

    def rows(kf, vf, misc, kd, vd, g, t):
        return (kf.reshape(1, g, t, h_fox, HEAD_DIM), vf.reshape(1, g, t, h_fox, HEAD_DIM),
                misc[:, MISC_F:MISC_F + h_fox].reshape(1, g, t, h_fox),
                kd.reshape(1, g, t, KV_DSA, HEAD_DIM), vd.reshape(1, g, t, KV_DSA, HEAD_DIM),
                misc[:, MISC_KI:MISC_KI + D_IDX].reshape(1, g, t, D_IDX))

    return ((y_p.reshape(b, s, d), y_s.reshape(db, dq, d))
            + rows(kf_p, vf_p, misc_p, kd_p, vd_p, b, s)
            + rows(kf_s, vf_s, misc_s, kd_s, vd_s, db, dq))
```

```python
import functools

import jax
import jax.numpy as jnp
from jax import lax
from jax.experimental import pallas as pl
from jax.experimental.pallas import tpu as pltpu

F32 = jnp.float32
BF16 = jnp.bfloat16
I32 = jnp.int32

HEAD_DIM = 128
D_IDX = 64
H_IDX = 16
KV_DSA = 2
TOPK_MAX = 256
N_GROUPS = 4
EXPERTS_PER_GROUP = 8
N_EXPERTS = N_GROUPS * EXPERTS_PER_GROUP
ROPE_THETA = 10000.0
EPS = 1e-6
PAGE_SIZE = 128

LANES = 128
NEG = -1e30
BISECT_STEPS = 36
VMEM_LIMIT = 52 * 1024 * 1024

MISC_KI = 0
MISC_F = 64
MISC_W = 72
MISC_END = 88


def _cp(sem):
    return pltpu.CompilerParams(dimension_semantics=sem, vmem_limit_bytes=VMEM_LIMIT)


def _dot(a, b):
    return jnp.dot(a, b, preferred_element_type=F32)


def _dot_nt(a, b):
    return lax.dot_general(a, b, (((1,), (1,)), ((), ())), preferred_element_type=F32)


def _adaln_kernel(c_ref, w_ref, b_ref, o_ref):
    c = c_ref[...]
    s = c / (1.0 + jnp.exp(-c))
    o_ref[...] = _dot(s.astype(BF16), w_ref[...].astype(BF16)) + b_ref[...]


def _adaln(c_all, w_ada, b_ada):
    rows, d = c_all.shape
    n = w_ada.shape[1]
    tn = 1024
    return pl.pallas_call(
        _adaln_kernel,
        grid=(n // tn,),
        in_specs=[pl.BlockSpec((rows, d), lambda j: (0, 0)),
                  pl.BlockSpec((d, tn), lambda j: (0, j)),
                  pl.BlockSpec((1, tn), lambda j: (0, j))],
        out_specs=pl.BlockSpec((rows, tn), lambda j: (0, j)),
        out_shape=jax.ShapeDtypeStruct((rows, n), F32),
        compiler_params=_cp(("parallel",)),
        name="adaln",
    )(c_all, w_ada, b_ada.reshape(1, n))


def _modulated(x, sc_ref, sh_ref):
    ms = jnp.mean(x * x, axis=-1, keepdims=True)
    return (x * lax.rsqrt(ms + EPS)) * (1.0 + sc_ref[...]) + sh_ref[...]


def _rms_heads(acc, gain, n_heads):
    outs = []
    for hh in range(n_heads):
        blk = acc[:, hh * HEAD_DIM:(hh + 1) * HEAD_DIM]
        ms = jnp.mean(blk * blk, axis=-1, keepdims=True)
        outs.append(blk * lax.rsqrt(ms + EPS) * gain)
    return outs


def _rope128(y, cos, sin_signed):
    return y * cos + pltpu.roll(y, HEAD_DIM // 2, axis=1) * sin_signed


def _rope64(y, cos, sin_signed, first_half):
    swapped = jnp.where(first_half, pltpu.roll(y, LANES - D_IDX // 2, axis=1),
                        pltpu.roll(y, D_IDX // 2, axis=1))
    return y * cos + swapped * sin_signed


def _proj_fox_kernel(x_ref, sc_ref, sh_ref, wq_ref, wk_ref, wv_ref, gq_ref, gk_ref,
                     qb_ref, kb_ref, vb_ref, k_ref, v_ref):
    h = _modulated(x_ref[...], sc_ref, sh_ref).astype(BF16)
    q = _rms_heads(_dot(h, wq_ref[...]), gq_ref[...], wq_ref.shape[1] // HEAD_DIM)
    for hh, blk in enumerate(q):
        qb_ref[:, hh * HEAD_DIM:(hh + 1) * HEAD_DIM] = blk.astype(BF16)
    k = _rms_heads(_dot(h, wk_ref[...]), gk_ref[...], wk_ref.shape[1] // HEAD_DIM)
    for hh, blk in enumerate(k):
        sl = slice(hh * HEAD_DIM, (hh + 1) * HEAD_DIM)
        k_ref[:, sl] = blk
        kb_ref[:, sl] = blk.astype(BF16)
    v = _dot(h, wv_ref[...])
    v_ref[...] = v
    vb_ref[...] = v.astype(BF16)


def _proj_dsa_kernel(x_ref, sc_ref, sh_ref, wq_ref, wk_ref, wv_ref, wi_ref, wm_ref,
                     gq_ref, gk_ref, bf_ref, c128_ref, s128_ref, c64_ref, s64_ref,
                     qb_ref, kb_ref, vb_ref, qi_ref, kia_ref, kib_ref, k_ref, v_ref, misc_ref):
    hf = _modulated(x_ref[...], sc_ref, sh_ref)
    h = hf.astype(BF16)

    def idx_dot(w_ref):
        if w_ref.dtype == F32:
            return jnp.dot(hf, w_ref[...], precision=lax.Precision.HIGHEST, preferred_element_type=F32)
        return _dot(h, w_ref[...])

    c128, s128 = c128_ref[...], s128_ref[...]
    c64, s64 = c64_ref[...], s64_ref[...]
    lane = lax.broadcasted_iota(I32, c64.shape, 1)
    first_half = (lane % D_IDX) < (D_IDX // 2)

    q = _rms_heads(_dot(h, wq_ref[...]), gq_ref[...], wq_ref.shape[1] // HEAD_DIM)
    for hh, blk in enumerate(q):
        qb_ref[:, hh * HEAD_DIM:(hh + 1) * HEAD_DIM] = _rope128(blk, c128, s128).astype(BF16)
    k = _rms_heads(_dot(h, wk_ref[...]), gk_ref[...], wk_ref.shape[1] // HEAD_DIM)
    for hh, blk in enumerate(k):
        sl = slice(hh * HEAD_DIM, (hh + 1) * HEAD_DIM)
        r = _rope128(blk, c128, s128)
        k_ref[:, sl] = r
        kb_ref[:, sl] = r.astype(BF16)
    v = _dot(h, wv_ref[...])
    v_ref[...] = v
    vb_ref[...] = v.astype(BF16)

    qi = idx_dot(wi_ref)
    for p in range(wi_ref.shape[1] // LANES):
        sl = slice(p * LANES, (p + 1) * LANES)
        qi_ref[:, sl] = _rope64(qi[:, sl], c64, s64, first_half).astype(qi_ref.dtype)

    m = idx_dot(wm_ref)
    roped = _rope64(m, c64, s64, first_half)
    xf = m + bf_ref[...]
    logsig = jnp.minimum(xf, 0.0) - jnp.log1p(jnp.exp(-jnp.abs(xf)))
    wsc = m * (H_IDX ** -0.5)
    ki_only = jnp.where(lane < MISC_F, roped, 0.0)
    misc_ref[...] = jnp.where(lane < MISC_F, roped,
                              jnp.where(lane < MISC_W, logsig,
                                        jnp.where(lane < MISC_END, wsc, 0.0)))
    kia_ref[...] = ki_only.astype(BF16)
    kib_ref[...] = pltpu.roll(ki_only, D_IDX, axis=1).astype(BF16)


def _row_specs(tm, d, rows_per_group, mod_rows):
    tiles_per_group = rows_per_group // tm
    x_spec = pl.BlockSpec((tm, d), lambda i: (i, 0))
    mod_spec = pl.BlockSpec((None, mod_rows, d), lambda i: (i // tiles_per_group, 0, 0))
    return x_spec, mod_spec


def _full(shape):
    nd = len(shape)
    return pl.BlockSpec(shape, lambda i: (0,) * nd)


def _proj_fox(x2d, sc, sh, wq, wk, wv, gq, gk, tm, rows_per_group):
    m, d = x2d.shape
    x_spec, mod_spec = _row_specs(tm, d, rows_per_group, sc.shape[1])
    n = wq.shape[1]
    o_spec = pl.BlockSpec((tm, n), lambda i: (i, 0))
    return pl.pallas_call(
        _proj_fox_kernel,
        grid=(m // tm,),
        in_specs=[x_spec, mod_spec, mod_spec, _full(wq.shape), _full(wk.shape), _full(wv.shape),
                  _full(gq.shape), _full(gk.shape)],
        out_specs=[o_spec] * 5,
        out_shape=[jax.ShapeDtypeStruct((m, n), BF16)] * 3 + [jax.ShapeDtypeStruct((m, n), F32)] * 2,
        compiler_params=_cp(("parallel",)),
        name="proj_fox",
    )(x2d, sc, sh, wq, wk, wv, gq, gk)


def _proj_dsa(x2d, sc, sh, wq, wk, wv, wi, wm, gq, gk, bf, tabs, tm, rows_per_group, tiles_per_seq):
    m, d = x2d.shape
    x_spec, mod_spec = _row_specs(tm, d, rows_per_group, sc.shape[1])
    tab_spec = pl.BlockSpec((tm, LANES), lambda i: (i % tiles_per_seq, 0))
    nq, nk, ni = wq.shape[1], wk.shape[1], wi.shape[1]

    def ospec(n):
        return pl.BlockSpec((tm, n), lambda i: (i, 0))

    def oshape(n, dt):
        return jax.ShapeDtypeStruct((m, n), dt)

    return pl.pallas_call(
        _proj_dsa_kernel,
        grid=(m // tm,),
        in_specs=[x_spec, mod_spec, mod_spec, _full(wq.shape), _full(wk.shape), _full(wv.shape),
                  _full(wi.shape), _full(wm.shape), _full(gq.shape), _full(gk.shape), _full(bf.shape),
                  tab_spec, tab_spec, tab_spec, tab_spec],
        out_specs=[ospec(nq), ospec(nk), ospec(nk), ospec(ni), ospec(LANES), ospec(LANES),
                   ospec(nk), ospec(nk), ospec(LANES)],
        out_shape=[oshape(nq, BF16), oshape(nk, BF16), oshape(nk, BF16), oshape(ni, wi.dtype),
                   oshape(LANES, BF16), oshape(LANES, BF16),
                   oshape(nk, F32), oshape(nk, F32), oshape(LANES, F32)],
        compiler_params=_cp(("parallel",)),
        name="proj_dsa",
    )(x2d, sc, sh, wq, wk, wv, wi, wm, gq, gk, bf, *tabs)


def _cumsum_kernel(x_ref, o_ref):
    x = x_ref[...]
    n = x.shape[1]
    lane = lax.broadcasted_iota(I32, x.shape, 1)
    k = 1
    while k < n:
        x = x + jnp.where(lane >= k, pltpu.roll(x, k, axis=1), 0.0)
        k *= 2
    o_ref[...] = x


def _cumsum_lanes(x):
    return pl.pallas_call(
        _cumsum_kernel,
        out_shape=jax.ShapeDtypeStruct(x.shape, F32),
        name="logf_cumsum",
    )(x)


def _fox_flash_kernel(q_ref, k_ref, v_ref, ck_ref, o_ref, s_ref, *, t, n_q):
    qi = pl.program_id(2)
    q = (q_ref[...].astype(F32) * (HEAD_DIM ** -0.5)).astype(BF16)

    def lane_fold(x, op, init):
        for j in range(x.shape[1] // LANES):
            init = op(init, x[:, j * LANES:(j + 1) * LANES])
        return init

    def attend(n_tiles):
        mx = jnp.full((t, LANES), -jnp.inf, F32)
        for c in range(n_tiles):
            s = _dot_nt(q, k_ref[c * t:(c + 1) * t, :]) - ck_ref[c]
            if c == n_tiles - 1:
                s = jnp.where(lax.broadcasted_iota(I32, s.shape, 1) <= lax.broadcasted_iota(I32, s.shape, 0),
                              s, -jnp.inf)
            s_ref[:, c * t:(c + 1) * t] = s
            mx = lane_fold(s, jnp.maximum, mx)
        m = jnp.max(mx, axis=-1, keepdims=True)
        ls = jnp.zeros((t, LANES), F32)
        acc = jnp.zeros((t, HEAD_DIM), F32)
        for c in range(n_tiles):
            p = jnp.exp(s_ref[:, c * t:(c + 1) * t] - m)
            ls = lane_fold(p, jnp.add, ls)
            acc = acc + _dot(p.astype(BF16), v_ref[c * t:(c + 1) * t, :])
        o_ref[...] = (acc / jnp.sum(ls, axis=-1, keepdims=True)).astype(o_ref.dtype)

    for n_tiles in range(1, n_q + 1):
        pl.when(qi == n_tiles - 1)(functools.partial(attend, n_tiles))


def _fox_prompt(qb, kb, vb, cum, t=512):
    b, s, hd = qb.shape
    h = hd // HEAD_DIM
    ck = cum.reshape(b, h, s // t, 1, t)
    kv_spec = pl.BlockSpec((None, s, HEAD_DIM), lambda bi, hi, qi: (bi, 0, hi))
    q_spec = pl.BlockSpec((None, t, HEAD_DIM), lambda bi, hi, qi: (bi, qi, hi))
    return pl.pallas_call(
        functools.partial(_fox_flash_kernel, t=t, n_q=s // t),
        grid=(b, h, s // t),
        in_specs=[q_spec, kv_spec, kv_spec,
                  pl.BlockSpec((None, None, s // t, 1, t), lambda bi, hi, qi: (bi, hi, 0, 0, 0))],
        out_specs=q_spec,
        out_shape=jax.ShapeDtypeStruct((b, s, hd), BF16),
        scratch_shapes=[pltpu.VMEM((t, s), F32)],
        compiler_params=_cp(("parallel", "parallel", "arbitrary")),
        name="fox_prompt",
    )(qb, kb, vb, ck)


def _order_key(x):
    bits = pltpu.bitcast(x + 0.0, I32)
    return jnp.where(bits < 0, bits ^ jnp.int32(0x7FFFFFFF), bits)


def _kth_largest_key(key, k, reduce_axes):
    shape = list(key.shape)
    for ax in reduce_axes:
        shape[ax] = 1
    sign = jnp.int32(-2 ** 31)

    def body(i, t):
        bit = lax.shift_left(jnp.int32(1), jnp.int32(31) - i)
        cand = t | bit
        ge = jnp.where(key >= (cand ^ sign), 1.0, 0.0)
        cnt = jnp.sum(ge, axis=reduce_axes, keepdims=True)
        return jnp.where(cnt >= float(k), cand, t)

    t = lax.fori_loop(0, 32, body, jnp.zeros(shape, I32))
    return t ^ sign


def _kth_largest_value(x, k, reduce_axes):
    kf = float(k)

    def count_ge(v):
        return jnp.sum(jnp.where(x >= v, 1.0, 0.0), axis=reduce_axes, keepdims=True)

    present = x > -jnp.inf
    top = jnp.max(x, axis=reduce_axes, keepdims=True)
    lo0 = jnp.min(jnp.where(present, x, jnp.inf), axis=reduce_axes, keepdims=True)

    def body(i, carry):
        lo, hi = carry
        mid = lo + 0.5 * (hi - lo)
        ge = count_ge(mid) >= kf
        return jnp.where(ge, mid, lo), jnp.where(ge, hi, mid)

    _, hi = lax.fori_loop(0, BISECT_STEPS, body, (lo0, top))
    below = jnp.max(jnp.where(x < hi, x, -jnp.inf), axis=reduce_axes, keepdims=True)
    thr = jnp.where(count_ge(top) >= kf, top, below)
    n_present = jnp.sum(jnp.where(present, 1.0, 0.0), axis=reduce_axes, keepdims=True)
    return jnp.where(n_present < kf, -jnp.inf, thr)


def _dsa_prompt_kernel(qi_ref, kia_ref, kib_ref, misc_ref, qd_ref, kd_ref, vd_ref, o_ref,
                       score_ref, selb_ref, *, tq, n_sel, q0):
    s_len = kia_ref.shape[0]
    row = (pl.program_id(1) + q0) * tq + lax.broadcasted_iota(I32, (tq, s_len), 0)
    col = lax.broadcasted_iota(I32, (tq, s_len), 1)
    causal = col <= row

    misc = misc_ref[...]
    kia, kib = kia_ref[...], kib_ref[...]
    acc = jnp.zeros((tq, s_len), F32)
    for p in range(qi_ref.shape[1] // LANES):
        qblk = qi_ref[:, p * LANES:(p + 1) * LANES]
        for half, kmat in enumerate((kia, kib)):
            hh = 2 * p + half
            w = misc[:, MISC_W + hh:MISC_W + hh + 1] * (D_IDX ** -0.5)
            acc = acc + w * jnp.maximum(_dot_nt(qblk, kmat), 0.0)
    score_ref[...] = jnp.where(causal, acc, -jnp.inf)

    def count(mask):
        return jnp.sum(jnp.where(mask, 1.0, 0.0), axis=1, keepdims=True)

    def select_ranked(vals, thr):
        need = float(n_sel) - count(vals > thr)
        ch = 256
        r_i = lax.broadcasted_iota(I32, (ch, ch), 0)
        c_i = lax.broadcasted_iota(I32, (ch, ch), 1)
        tri = jnp.where(r_i < c_i, 1.0, 0.0).astype(BF16)
        base = jnp.zeros((tq, 1), F32)
        for c in range(s_len // ch):
            sl = slice(c * ch, (c + 1) * ch)
            eq_c = jnp.where(vals[:, sl] == thr, 1.0, 0.0)
            rank = base + _dot(eq_c.astype(BF16), tri)
            take = (vals[:, sl] > thr) | ((eq_c > 0.0) & (rank < need))
            selb_ref[:, sl] = jnp.where(take & causal[:, sl], 0.0, -jnp.inf)
            base = base + jnp.sum(eq_c, axis=1, keepdims=True)

    score = score_ref[...]
    thr = _kth_largest_value(score, n_sel, (1,))
    cnt_ge = count(score >= thr)
    selb_ref[...] = jnp.where((score >= thr) & causal, 0.0, -jnp.inf)
    finite_thr = thr > -jnp.inf
    tie_rows = jnp.where((cnt_ge > float(n_sel)) & finite_thr, 1.0, 0.0)
    unresolved = jnp.where((cnt_ge < float(n_sel)) & finite_thr, 1.0, 0.0)

    @pl.when(jnp.max(tie_rows) > 0.0)
    def _():
        select_ranked(score, thr)

    @pl.when(jnp.max(unresolved) > 0.0)
    def _():
        key = _order_key(score)
        select_ranked(key, _kth_largest_key(key, n_sel, (1,)))

    n_heads = qd_ref.shape[1] // HEAD_DIM
    rep = n_heads // KV_DSA
    for hh in range(n_heads):
        g = hh // rep
        q = qd_ref[:, hh * HEAD_DIM:(hh + 1) * HEAD_DIM]
        kg = kd_ref[:, g * HEAD_DIM:(g + 1) * HEAD_DIM]
        vg = vd_ref[:, g * HEAD_DIM:(g + 1) * HEAD_DIM]
        s = _dot_nt(q, kg) * (HEAD_DIM ** -0.5) + selb_ref[...]
        m = jnp.max(s, axis=-1, keepdims=True)
        p = jnp.exp(s - m)
        l = jnp.sum(p, axis=-1, keepdims=True)
        o_ref[:, hh * HEAD_DIM:(hh + 1) * HEAD_DIM] = (_dot(p.astype(BF16), vg) / l).astype(o_ref.dtype)


def _dsa_prompt(qi, kia, kib, misc, qd, kd, vd, tq=256, tiles_per_call=1):
    b, s, _ = qi.shape
    n_sel = min(TOPK_MAX, s // 4)
    outs = []
    for q0 in range(0, s // tq, tiles_per_call):
        s_eff = (q0 + tiles_per_call) * tq

        def qspec(n, q0=q0):
            return pl.BlockSpec((None, tq, n), lambda bi, i: (bi, i + q0, 0))

        def kspec(n, s_eff=s_eff):
            return pl.BlockSpec((None, s_eff, n), lambda bi, i: (bi, 0, 0))

        outs.append(pl.pallas_call(
            functools.partial(_dsa_prompt_kernel, tq=tq, n_sel=n_sel, q0=q0),
            grid=(b, tiles_per_call),
            in_specs=[qspec(qi.shape[2]), kspec(LANES), kspec(LANES), qspec(LANES),
                      qspec(qd.shape[2]), kspec(kd.shape[2]), kspec(vd.shape[2])],
            out_specs=pl.BlockSpec((None, tq, qd.shape[2]), lambda bi, i: (bi, i, 0)),
            out_shape=jax.ShapeDtypeStruct((b, tiles_per_call * tq, qd.shape[2]), BF16),
            scratch_shapes=[pltpu.VMEM((tq, s_eff), F32), pltpu.VMEM((tq, s_eff), F32)],
            compiler_params=_cp(("parallel", "parallel")),
            name="dsa_prompt",
        )(qi, kia, kib, misc, qd, kd, vd))
    return jnp.concatenate(outs, axis=1)


def _idx_score_rows(qi, w, kt_refs):
    rows = []
    for kt_ref in kt_refs:
        d = jnp.dot(qi, kt_ref[...], precision=lax.Precision.HIGHEST, preferred_element_type=F32)
        rows.append(jnp.sum(w * jnp.maximum(d, 0.0), axis=0, keepdims=True))
    return jnp.concatenate(rows, axis=0)


def _fox_sample_kernel(pt_ref, q_ref, kn_ref, vn_ref, lfn_ref, qi_ref, wi_ref, kin_ref, *refs, pp, n_heads):
    k_refs = refs[:pp]
    v_refs = refs[pp:2 * pp]
    lf_refs = refs[2 * pp:3 * pp]
    kt_refs = refs[3 * pp:4 * pp]
    o_ref, sc_ref, scn_ref, m_ref, l_ref, acc_ref, carry_ref = refs[4 * pp:]
    c = pl.program_id(1)
    width = PAGE_SIZE * n_heads
    q = q_ref[...] * (HEAD_DIM ** -0.5)
    qb = q.astype(BF16)
    qi = qi_ref[...]
    wi = wi_ref[...] * (D_IDX ** -0.5)
    own_head = (lax.broadcasted_iota(I32, (n_heads, width), 1) % n_heads
                == lax.broadcasted_iota(I32, (n_heads, width), 0))

    @pl.when(c == 0)
    def _():
        s_new = jnp.sum(q * kn_ref[...], axis=-1, keepdims=True)
        m_ref[...] = jnp.broadcast_to(s_new, m_ref.shape)
        l_ref[...] = jnp.ones(l_ref.shape, F32)
        acc_ref[...] = vn_ref[...]
        carry_ref[...] = lfn_ref[...]
        d = jnp.sum(qi * kin_ref[...], axis=-1, keepdims=True)
        sn = jnp.sum(wi * jnp.maximum(d, 0.0), axis=0, keepdims=True)
        r = lax.broadcasted_iota(I32, scn_ref.shape, 0)
        ln = lax.broadcasted_iota(I32, scn_ref.shape, 1)
        scn_ref[...] = jnp.where((r == 0) & (ln == 0), jnp.broadcast_to(sn, scn_ref.shape), -jnp.inf)

    sc_ref[...] = _idx_score_rows(qi, wi, kt_refs)

    lf = jnp.concatenate([r[...] for r in lf_refs], axis=0)
    lane = lax.broadcasted_iota(I32, lf.shape, 1)
    tot, suf = lf, lf
    k = n_heads
    while k < width:
        tot = tot + pltpu.roll(tot, k, axis=1)
        suf = suf + jnp.where(lane + k < width, pltpu.roll(suf, width - k, axis=1), 0.0)
        k *= 2
    later = suf - lf
    run = carry_ref[...]
    bias = [None] * pp
    for j in reversed(range(pp)):
        bias[j] = run + later[j:j + 1, :]
        run = run + tot[j:j + 1, :]
    carry_ref[...] = run

    logits = []
    for j in range(pp):
        s = _dot_nt(qb, k_refs[j][...].astype(BF16))
        logits.append(jnp.where(own_head, s + bias[j], NEG))
    m_prev = m_ref[...]
    m_new = m_prev
    for j in range(pp):
        m_new = jnp.maximum(m_new, jnp.max(logits[j], axis=-1, keepdims=True))
    alpha = jnp.exp(m_prev - m_new)
    l_new = alpha * l_ref[...]
    acc = alpha * acc_ref[...]
    for j in range(pp):
        p = jnp.exp(logits[j] - m_new[:, :1])
        l_new = l_new + jnp.sum(p, axis=-1, keepdims=True)
        acc = acc + _dot(p.astype(BF16), v_refs[j][...].astype(BF16))
    m_ref[...] = m_new
    l_ref[...] = l_new
    acc_ref[...] = acc

    @pl.when(c == pl.num_programs(1) - 1)
    def _():
        o_ref[...] = acc / l_new


def _fox_sample(page_table, q, k_new, v_new, lf_new, qi, wi, ki_new, cache_k, cache_v, cache_lf, cache_idx_t, pp=8):
    db, n_heads, _ = q.shape
    n_pages = page_table.shape[1]
    nc = n_pages // pp
    width = PAGE_SIZE * n_heads

    def page_idx(j):
        return lambda bi, ci, pt: (pt[bi, (nc - 1 - ci) * pp + j], 0, 0)

    def per_seq(shape):
        return pl.BlockSpec((None,) + shape, lambda bi, ci, pt: (bi, 0, 0))

    vec_spec = per_seq((n_heads, HEAD_DIM))
    kv_specs = [pl.BlockSpec((None, width, HEAD_DIM), page_idx(j)) for j in range(pp)]
    lf_specs = [pl.BlockSpec((None, 1, width), page_idx(j)) for j in range(pp)]
    kt_specs = [pl.BlockSpec((None, D_IDX, PAGE_SIZE), page_idx(j)) for j in range(pp)]
    grid_spec = pltpu.PrefetchScalarGridSpec(
        num_scalar_prefetch=1,
        grid=(db, nc),
        in_specs=[vec_spec, vec_spec, vec_spec, per_seq((1, width)),
                  per_seq((H_IDX, D_IDX)), per_seq((H_IDX, LANES)), per_seq((1, D_IDX))]
                 + kv_specs + kv_specs + lf_specs + kt_specs,
        out_specs=[vec_spec,
                   pl.BlockSpec((None, pp, PAGE_SIZE), lambda bi, ci, pt: (bi, nc - 1 - ci, 0)),
                   per_seq((8, LANES))],
        scratch_shapes=[pltpu.VMEM((n_heads, HEAD_DIM), F32)] * 3 + [pltpu.VMEM((1, width), F32)],
    )
    return pl.pallas_call(
        functools.partial(_fox_sample_kernel, pp=pp, n_heads=n_heads),
        grid_spec=grid_spec,
        out_shape=[jax.ShapeDtypeStruct((db, n_heads, HEAD_DIM), F32),
                   jax.ShapeDtypeStruct((db, n_pages, PAGE_SIZE), F32),
                   jax.ShapeDtypeStruct((db, 8, LANES), F32)],
        compiler_params=_cp(("parallel", "arbitrary")),
        name="fox_sample",
    )(page_table, q, k_new, v_new, lf_new, qi, wi, ki_new,
      *([cache_k] * pp), *([cache_v] * pp), *([cache_lf] * pp), *([cache_idx_t] * pp))


def _select_kernel(s_ref, past_ref, new_ref, take_ref, *, n_sel, n_pages):
    score = s_ref[...]
    db, rows, _ = score.shape
    valid = score > -jnp.inf

    def count(mask):
        return jnp.sum(jnp.where(mask, 1.0, 0.0), axis=(1, 2), keepdims=True)

    def select_ranked(vals, thr):
        need = float(n_sel) - count(vals > thr)
        eq = jnp.where(vals == thr, 1.0, 0.0)
        r_i = lax.broadcasted_iota(I32, (LANES, LANES), 0)
        c_i = lax.broadcasted_iota(I32, (LANES, LANES), 1)
        tri = jnp.where(r_i < c_i, 1.0, 0.0).astype(BF16)
        within = _dot(eq.reshape(db * rows, LANES).astype(BF16), tri).reshape(db, rows, LANES)
        row_cnt = jnp.broadcast_to(jnp.sum(eq, axis=2, keepdims=True), score.shape)
        rows_pad = 2 * LANES
        cnt_pad = jnp.concatenate([row_cnt, jnp.zeros((db, rows_pad - rows, LANES), F32)], axis=1)
        rr = lax.broadcasted_iota(I32, (db, rows_pad, rows_pad), 1)
        rc = lax.broadcasted_iota(I32, (db, rows_pad, rows_pad), 2)
        before = jnp.einsum("bij,bjl->bil", jnp.where(rc < rr, 1.0, 0.0).astype(BF16), cnt_pad.astype(BF16),
                            preferred_element_type=F32)[:, :rows, :]
        take = (vals > thr) | ((eq > 0.0) & (within + before < need))
        take_ref[...] = jnp.where(take & valid, 1.0, 0.0)

    thr = _kth_largest_value(score, n_sel, (1, 2))
    cnt_ge = count(score >= thr)
    take_ref[...] = jnp.where((score >= thr) & valid, 1.0, 0.0)
    finite_thr = thr > -jnp.inf
    ties = jnp.where((cnt_ge > float(n_sel)) & finite_thr, 1.0, 0.0)
    unresolved = jnp.where((cnt_ge < float(n_sel)) & finite_thr, 1.0, 0.0)

    @pl.when(jnp.max(ties) > 0.0)
    def _():
        select_ranked(score, thr)

    @pl.when(jnp.max(unresolved) > 0.0)
    def _():
        key = _order_key(score)
        select_ranked(key, _kth_largest_key(key, n_sel, (1, 2)))

    take01 = take_ref[...]
    new_ref[...] = jnp.where(take01[:, n_pages:, :] > 0.5, 0.0, NEG)
    t_i = lax.broadcasted_iota(I32, (LANES, LANES * KV_DSA), 0)
    l_i = lax.broadcasted_iota(I32, (LANES, LANES * KV_DSA), 1)
    spread = jnp.where(l_i // KV_DSA == t_i, 1.0, 0.0).astype(BF16)
    past = _dot(take01[:, :n_pages, :].reshape(db * n_pages, LANES).astype(BF16), spread)
    past_ref[...] = jnp.where(past > 0.5, 0.0, NEG).reshape(db, n_pages, LANES * KV_DSA)


def _select_sample(scores, n_sel, n_pages):
    db, rows, lanes = scores.shape
    return pl.pallas_call(
        functools.partial(_select_kernel, n_sel=n_sel, n_pages=n_pages),
        out_shape=[jax.ShapeDtypeStruct((db, n_pages, lanes * KV_DSA), F32),
                   jax.ShapeDtypeStruct((db, rows - n_pages, lanes), F32)],
        scratch_shapes=[pltpu.VMEM((db, rows, lanes), F32)],
        compiler_params=pltpu.CompilerParams(vmem_limit_bytes=VMEM_LIMIT),
        name="select_sample",
    )(scores)


def _dsa_sample_kernel(pt_ref, q_ref, kn_ref, vn_ref, selb_ref, selbn_ref, *refs, pp, n_heads):
    k_refs = refs[:pp]
    v_refs = refs[pp:2 * pp]
    o_ref, m_ref, l_ref, acc_ref = refs[2 * pp:]
    c = pl.program_id(1)
    rep = n_heads // KV_DSA
    width = PAGE_SIZE * KV_DSA
    q = q_ref[...] * (HEAD_DIM ** -0.5)
    qb = q.astype(BF16)
    grp = lax.broadcasted_iota(I32, (n_heads, HEAD_DIM), 0) // rep
    own_kv = (lax.broadcasted_iota(I32, (n_heads, width), 1) % KV_DSA
              == lax.broadcasted_iota(I32, (n_heads, width), 0) // rep)

    @pl.when(c == 0)
    def _():
        kn = kn_ref[...]
        vn = vn_ref[...]
        s_new = jnp.zeros((n_heads, 1), F32)
        v_rows = jnp.zeros((n_heads, HEAD_DIM), F32)
        for g in range(KV_DSA):
            sg = jnp.sum(q * kn[g:g + 1, :], axis=-1, keepdims=True)
            s_new = jnp.where(grp[:, :1] == g, sg, s_new)
            v_rows = jnp.where(grp == g, jnp.broadcast_to(vn[g:g + 1, :], v_rows.shape), v_rows)
        s_new = s_new + selbn_ref[0:1, 0:1]
        m_ref[...] = jnp.broadcast_to(jnp.maximum(s_new, NEG), m_ref.shape)
        p_new = jnp.exp(s_new - m_ref[:, :1])
        l_ref[...] = jnp.broadcast_to(p_new, l_ref.shape)
        acc_ref[...] = p_new * v_rows

    selb = selb_ref[...]
    logits = []
    for j in range(pp):
        s = _dot_nt(qb, k_refs[j][...].astype(BF16))
        logits.append(jnp.where(own_kv, s + selb[j:j + 1, :], NEG))

    m_prev = m_ref[...]
    m_new = m_prev
    for j in range(pp):
        m_new = jnp.maximum(m_new, jnp.max(logits[j], axis=-1, keepdims=True))
    alpha = jnp.exp(m_prev - m_new)
    l_new = alpha * l_ref[...]
    acc = alpha * acc_ref[...]
    for j in range(pp):
        p = jnp.exp(logits[j] - m_new[:, :1])
        l_new = l_new + jnp.sum(p, axis=-1, keepdims=True)
        acc = acc + _dot(p.astype(BF16), v_refs[j][...].astype(BF16))
    m_ref[...] = m_new
    l_ref[...] = l_new
    acc_ref[...] = acc

    @pl.when(c == pl.num_programs(1) - 1)
    def _():
        o_ref[...] = acc / l_new


def _dsa_sample(page_table, q, k_new, v_new, selb_past, selb_new, cache_k, cache_v, pp=16):
    db, n_heads, _ = q.shape
    n_pages = page_table.shape[1]
    nc = n_pages // pp
    width = PAGE_SIZE * KV_DSA

    def page_idx(j):
        return lambda bi, ci, pt: (pt[bi, ci * pp + j], 0, 0)

    q_spec = pl.BlockSpec((None, n_heads, HEAD_DIM), lambda bi, ci, pt: (bi, 0, 0))
    n_spec = pl.BlockSpec((None, KV_DSA, HEAD_DIM), lambda bi, ci, pt: (bi, 0, 0))
    kv_specs = [pl.BlockSpec((None, width, HEAD_DIM), page_idx(j)) for j in range(pp)]
    grid_spec = pltpu.PrefetchScalarGridSpec(
        num_scalar_prefetch=1,
        grid=(db, nc),
        in_specs=[q_spec, n_spec, n_spec,
                  pl.BlockSpec((None, pp, width), lambda bi, ci, pt: (bi, ci, 0)),
                  pl.BlockSpec((None,) + selb_new.shape[1:], lambda bi, ci, pt: (bi, 0, 0))]
                 + kv_specs + kv_specs,
        out_specs=q_spec,
        scratch_shapes=[pltpu.VMEM((n_heads, HEAD_DIM), F32)] * 3,
    )
    return pl.pallas_call(
        functools.partial(_dsa_sample_kernel, pp=pp, n_heads=n_heads),
        grid_spec=grid_spec,
        out_shape=jax.ShapeDtypeStruct((db, n_heads, HEAD_DIM), F32),
        compiler_params=_cp(("parallel", "arbitrary")),
        name="dsa_sample",
    )(page_table, q, k_new, v_new, selb_past, selb_new, *([cache_k] * pp), *([cache_v] * pp))


def _outproj_kernel(x_ref, of_ref, od_ref, wf_ref, wd_ref, ga_ref, sc_ref, sh_ref, wr_ref,
                    x1_ref, h2_ref, route_ref):
    def proj(a_ref, w_ref):
        if w_ref.dtype == F32:
            return jnp.dot(a_ref[...].astype(F32), w_ref[...], precision=lax.Precision.HIGHEST,
                           preferred_element_type=F32)
        return _dot(a_ref[...].astype(BF16), w_ref[...])

    mix = proj(of_ref, wf_ref) + proj(od_ref, wd_ref)
    x1 = x_ref[...] + ga_ref[...] * mix
    x1_ref[...] = x1
    h2 = _modulated(x1, sc_ref, sh_ref)
    h2_ref[...] = h2
    if wr_ref.dtype == F32:
        logits = jnp.dot(h2, wr_ref[...], precision=lax.Precision.HIGHEST, preferred_element_type=F32)
    else:
        logits = _dot(h2.astype(BF16), wr_ref[...])
    lane = lax.broadcasted_iota(I32, logits.shape, 1).astype(F32)

    def first_argmax(vals, mask):
        v = jnp.where(mask, vals, -jnp.inf)
        mx = jnp.max(v, axis=-1, keepdims=True)
        idx = jnp.min(jnp.where(v == mx, lane, float(LANES)), axis=-1, keepdims=True)
        return mx, idx

    is_g = lane < float(N_GROUPS)
    mg, g_sel = first_argmax(logits, is_g)
    p_gsel = 1.0 / jnp.sum(jnp.where(is_g, jnp.exp(logits - mg), 0.0), axis=-1, keepdims=True)
    lo = float(N_GROUPS) + g_sel * float(EXPERTS_PER_GROUP)
    in_grp = (lane >= lo) & (lane < lo + float(EXPERTS_PER_GROUP))
    m1, i1 = first_argmax(logits, in_grp)
    m2, i2 = first_argmax(logits, in_grp & (lane != i1))
    e2 = jnp.exp(m2 - m1)
    w1 = p_gsel / (1.0 + e2)
    w2 = p_gsel * e2 / (1.0 + e2)
    route_ref[...] = jnp.where(lane == 0.0, i1 - float(N_GROUPS),
                               jnp.where(lane == 1.0, i2 - float(N_GROUPS),
                                         jnp.where(lane == 2.0, w1, jnp.where(lane == 3.0, w2, 0.0))))


def _outproj(x2d, o_fox, o_dsa, w_out, ga, sc, sh, wr, tm, rows_per_group):
    m, d = x2d.shape
    x_spec, mod_spec = _row_specs(tm, d, rows_per_group, sc.shape[1])
    half = o_fox.shape[1]
    half_spec = pl.BlockSpec((tm, half), lambda i: (i, 0))
    wf, wd = w_out, w_out
    return pl.pallas_call(
        _outproj_kernel,
        grid=(m // tm,),
        in_specs=[x_spec, half_spec, half_spec,
                  pl.BlockSpec((half, d), lambda i: (0, 0)), pl.BlockSpec((half, d), lambda i: (1, 0)),
                  mod_spec, mod_spec, mod_spec, _full(wr.shape)],
        out_specs=[x_spec, x_spec, pl.BlockSpec((tm, LANES), lambda i: (i, 0))],
        out_shape=[jax.ShapeDtypeStruct((m, d), F32), jax.ShapeDtypeStruct((m, d), F32),
                   jax.ShapeDtypeStruct((m, LANES), F32)],
        compiler_params=_cp(("parallel",)),
        name="outproj_router",
    )(x2d, o_fox, o_dsa, wf, wd, ga, sc, sh, wr)


def _dispatch_kernel(ends_ref, s0_ref, s1_ref, h_ref, t0_ref, t1_ref, ht_ref, xs_hbm, zbuf, sem, zsem,
                     *, tm, n_main):
    i = pl.program_id(0)

    @pl.when(i == 0)
    def _():
        zbuf[...] = jnp.zeros(zbuf.shape, F32)

        def fill(e):
            start = pl.multiple_of(jnp.maximum(ends_ref[0, e] - tm, 0), tm)
            return pltpu.make_async_copy(zbuf, xs_hbm.at[pl.ds(start, tm)], zsem)

        for e in range(N_EXPERTS):
            fill(e).start()
        for e in range(N_EXPERTS):
            fill(e).wait()

        def clear_unused(k, carry):
            cp = pltpu.make_async_copy(zbuf, xs_hbm.at[pl.ds(pl.multiple_of(k * tm, tm), tm)], zsem)
            cp.start()
            cp.wait()
            return carry

        lax.fori_loop(ends_ref[0, N_EXPERTS - 1] // tm, xs_hbm.shape[0] // tm, clear_unused, 0)

    def scatter(a_ref, b_ref, src_ref, rows):
        def issue(r, carry):
            src = src_ref.at[pl.ds(r, 1)]
            pltpu.make_async_copy(src, xs_hbm.at[pl.ds(a_ref[0, r], 1)], sem).start(priority=0)
            pltpu.make_async_copy(src, xs_hbm.at[pl.ds(b_ref[0, r], 1)], sem).start(priority=1)
            return carry

        lax.fori_loop(0, rows, issue, 0, unroll=8)
        pltpu.make_async_copy(src_ref, xs_hbm.at[pl.ds(0, rows)], sem).wait()
        pltpu.make_async_copy(src_ref, xs_hbm.at[pl.ds(0, rows)], sem).wait()

    @pl.when(i < n_main)
    def _():
        scatter(s0_ref, s1_ref, h_ref, tm)

    @pl.when(i == n_main)
    def _():
        scatter(t0_ref, t1_ref, ht_ref, ht_ref.shape[0])


def _dispatch(ends, slots_main, h_main, slots_tail, h_tail, n_slots, tm):
    m, d = h_main.shape
    mt = h_tail.shape[0]
    n_main = m // tm
    smem = pltpu.SMEM

    def main_idx(i):
        return (jnp.minimum(i, n_main - 1), 0, 0)

    id_spec = pl.BlockSpec((None, 1, tm), main_idx, memory_space=smem)
    tail_spec = pl.BlockSpec((None, 1, mt), lambda i: (0, 0, 0), memory_space=smem)
    return pl.pallas_call(
        functools.partial(_dispatch_kernel, tm=tm, n_main=n_main),
        grid=(n_main + 1,),
        in_specs=[pl.BlockSpec(memory_space=smem), id_spec, id_spec,
                  pl.BlockSpec((tm, d), lambda i: (jnp.minimum(i, n_main - 1), 0)),
                  tail_spec, tail_spec, pl.BlockSpec((mt, d), lambda i: (0, 0))],
        out_specs=pl.BlockSpec(memory_space=pl.ANY),
        out_shape=jax.ShapeDtypeStruct((n_slots, d), F32),
        scratch_shapes=[pltpu.VMEM((tm, d), F32), pltpu.SemaphoreType.DMA(()), pltpu.SemaphoreType.DMA(())],
        compiler_params=_cp(("arbitrary",)),
        name="moe_dispatch",
    )(ends.reshape(1, N_EXPERTS),
      slots_main[:, 0].reshape(n_main, 1, tm), slots_main[:, 1].reshape(n_main, 1, tm), h_main,
      slots_tail[:, 0].reshape(1, 1, mt), slots_tail[:, 1].reshape(1, 1, mt), h_tail)


def _moe_kernel(te_ref, na_ref, x_ref, wg_ref, wu_ref, wd_ref, o_ref, wg_b, wu_b, wd_b):
    t = pl.program_id(0)

    @pl.when(t < na_ref[0])
    def _():
        @pl.when((t == 0) | (te_ref[t] != te_ref[jnp.maximum(t - 1, 0)]))
        def _():
            wg_b[...] = wg_ref[...].astype(BF16)
            wu_b[...] = wu_ref[...].astype(BF16)
            wd_b[...] = wd_ref[...].astype(BF16)

        h = x_ref[...].astype(BF16)
        a = _dot(h, wg_b[...])
        u = _dot(h, wu_b[...])
        act = (a / (1.0 + jnp.exp(-a))) * u
        o_ref[...] = _dot(act.astype(BF16), wd_b[...])

    @pl.when(t >= na_ref[0])
    def _():
        o_ref[...] = jnp.zeros(o_ref.shape, F32)


def _moe_experts(tile_expert, n_active, x_sorted, w_gate, w_up, w_down, tm):
    n_tiles = tile_expert.shape[0]
    d = x_sorted.shape[1]
    f = w_gate.shape[2]

    def row_idx(t, te, na):
        return (jnp.minimum(t, na[0] - 1), 0)

    def w_idx(t, te, na):
        return (te[jnp.minimum(t, na[0] - 1)], 0, 0)

    grid_spec = pltpu.PrefetchScalarGridSpec(
        num_scalar_prefetch=2,
        grid=(n_tiles,),
        in_specs=[pl.BlockSpec((tm, d), row_idx),
                  pl.BlockSpec((None, d, f), w_idx),
                  pl.BlockSpec((None, d, f), w_idx),
                  pl.BlockSpec((None, f, d), w_idx)],
        out_specs=pl.BlockSpec((tm, d), lambda t, te, na: (t, 0)),
        scratch_shapes=[pltpu.VMEM((d, f), BF16), pltpu.VMEM((d, f), BF16), pltpu.VMEM((f, d), BF16)],
    )
    return pl.pallas_call(
        _moe_kernel,
        grid_spec=grid_spec,
        out_shape=jax.ShapeDtypeStruct((n_tiles * tm, d), F32),
        compiler_params=_cp(("arbitrary",)),
        name="moe_experts",
    )(tile_expert, n_active, x_sorted, w_gate, w_up, w_down)


def _combine_kernel(s0_ref, s1_ref, x1_ref, gm_ref, route_ref, y_hbm, o_ref, b0, b1, sem, *, tm):
    def copies(r):
        return (pltpu.make_async_copy(y_hbm.at[pl.ds(s0_ref[0, r], 1)], b0.at[pl.ds(r, 1)], sem),
                pltpu.make_async_copy(y_hbm.at[pl.ds(s1_ref[0, r], 1)], b1.at[pl.ds(r, 1)], sem))

    def issue(r, carry):
        c0, c1 = copies(r)
        c0.start(priority=0)
        c1.start(priority=1)
        return carry

    lax.fori_loop(0, tm, issue, 0, unroll=8)
    pltpu.make_async_copy(y_hbm.at[pl.ds(0, tm)], b0, sem).wait()
    pltpu.make_async_copy(y_hbm.at[pl.ds(0, tm)], b1, sem).wait()
    route = route_ref[...]
    moe = route[:, 2:3] * b0[...] + route[:, 3:4] * b1[...]
    o_ref[...] = x1_ref[...] + gm_ref[...] * moe


def _combine(slot0, slot1, x1, gm, route, y_slots, tm, rows_per_group):
    m, d = x1.shape
    x_spec, mod_spec = _row_specs(tm, d, rows_per_group, gm.shape[1])
    n_tiles = m // tm
    id_spec = pl.BlockSpec((None, 1, tm), lambda i: (i, 0, 0), memory_space=pltpu.SMEM)
    return pl.pallas_call(
        functools.partial(_combine_kernel, tm=tm),
        grid=(n_tiles,),
        in_specs=[id_spec, id_spec, x_spec, mod_spec,
                  pl.BlockSpec((tm, LANES), lambda i: (i, 0)),
                  pl.BlockSpec(memory_space=pl.ANY)],
        out_specs=x_spec,
        out_shape=jax.ShapeDtypeStruct((m, d), F32),
        scratch_shapes=[pltpu.VMEM((tm, d), F32), pltpu.VMEM((tm, d), F32), pltpu.SemaphoreType.DMA(())],
        compiler_params=_cp(("arbitrary",)),
        name="moe_combine",
    )(slot0.reshape(n_tiles, 1, tm), slot1.reshape(n_tiles, 1, tm), x1, gm, route, y_slots)


def _moe_plan(expert_ids, tm):
    t_all = expert_ids.shape[0]
    flat = expert_ids.reshape(-1)
    onehot = (flat[:, None] == jnp.arange(N_EXPERTS, dtype=I32)[None, :]).astype(I32)
    rank = jnp.sum((jnp.cumsum(onehot, axis=0) - onehot) * onehot, axis=1)
    counts = jnp.sum(onehot, axis=0)
    padded = ((counts + tm - 1) // tm) * tm
    ends = jnp.cumsum(padded).astype(I32)
    starts = ends - padded
    slot = jnp.sum(onehot * starts[None, :], axis=1) + rank
    n_tiles = (2 * t_all + N_EXPERTS * (tm - 1)) // tm + 1
    tile_start = jnp.arange(n_tiles, dtype=I32) * tm
    tile_expert = jnp.minimum(jnp.sum((tile_start[:, None] >= ends[None, :]).astype(I32), axis=1),
                              N_EXPERTS - 1)
    n_active = (ends[-1:] // tm).astype(I32)
    return tile_expert.astype(I32), n_active, ends, slot.reshape(t_all, 2).astype(I32)


def _rope_tables(pos, dim, lanes):
    half = dim // 2
    inv = ROPE_THETA ** (-jnp.arange(half, dtype=F32) / half)
    ang = pos.astype(F32)[:, None] * inv[None, :]
    cos, sin = jnp.cos(ang), jnp.sin(ang)
    reps = lanes // dim
    cos_t = jnp.tile(jnp.concatenate([cos, cos], axis=-1), (1, reps))
    sin_t = jnp.tile(jnp.concatenate([-sin, sin], axis=-1), (1, reps))
    return cos_t, sin_t


def kernel(x_prompt, x_sample, cache_fox_k, cache_fox_v, cache_fox_logf, cache_dsa_k, cache_dsa_v,
           cache_idx_k, page_table, c_prompt, c_sample, w_in, b_forget, q_gain_fox, k_gain_fox,
           q_gain_dsa, k_gain_dsa, w_out, w_ada, b_ada, w_router_group, w_router_expert,
           w_gate, w_up, w_down):
    b, s, d = x_prompt.shape
    db, dq, _ = x_sample.shape
    depth = w_in.shape[0]
    assert depth == 1 and dq == 1
    past_len = page_table.shape[1] * PAGE_SIZE
    n_phys = cache_fox_k.shape[1]
    h_fox = cache_fox_k.shape[3]
    h_dsa = d // (2 * HEAD_DIM)
    t_p, t_s = b * s, db * dq
    t_all = t_p + t_s
    tm_p, tm_s = 256, t_s
    layer = 0

    n_c = b + db
    pad = (-n_c) % 8
    c_all = jnp.concatenate([c_prompt, c_sample, jnp.zeros((pad, d), F32)], axis=0)
    mod = _adaln(c_all, w_ada[layer], b_ada[layer])
    mods = [mod[:, i * d:(i + 1) * d] for i in range(6)]
    mod_p = [m_[:b].reshape(b, 1, d) for m_ in mods]
    mod_s = [m_[b:b + db].reshape(1, db, d) for m_ in mods]

    w = w_in[layer]
    nf = h_fox * HEAD_DIM
    nd = h_dsa * HEAD_DIM
    nkv = KV_DSA * HEAD_DIM
    ni = H_IDX * D_IDX
    o = 0
    wqf = w[:, o:o + nf].astype(BF16); o += nf
    wkf = w[:, o:o + nf].astype(BF16); o += nf
    wvf = w[:, o:o + nf].astype(BF16); o += nf
    w_fl = w[:, o:o + h_fox]; o += h_fox
    wqd = w[:, o:o + nd].astype(BF16); o += nd
    wkd = w[:, o:o + nkv].astype(BF16); o += nkv
    wvd = w[:, o:o + nkv].astype(BF16); o += nkv
    wqi = w[:, o:o + ni]; o += ni
    w_ki = w[:, o:o + D_IDX]; o += D_IDX
    w_wi = w[:, o:o + H_IDX]; o += H_IDX
    assert h_fox <= MISC_W - MISC_F
    wmisc = jnp.concatenate([w_ki, w_fl, jnp.zeros((d, MISC_W - MISC_F - h_fox), F32), w_wi,
                             jnp.zeros((d, LANES - MISC_END), F32)], axis=1)
    bf = jnp.zeros((1, LANES), F32).at[0, MISC_F:MISC_F + h_fox].set(b_forget[layer])
    gqf, gkf = q_gain_fox[layer].reshape(1, HEAD_DIM), k_gain_fox[layer].reshape(1, HEAD_DIM)
    gqd, gkd = q_gain_dsa[layer].reshape(1, HEAD_DIM), k_gain_dsa[layer].reshape(1, HEAD_DIM)

    pos_p = jnp.arange(s, dtype=I32)
    pos_s = jnp.full((t_s,), past_len, I32)
    tabs_p = _rope_tables(pos_p, HEAD_DIM, LANES) + _rope_tables(pos_p, D_IDX, LANES)
    tabs_s = _rope_tables(pos_s, HEAD_DIM, LANES) + _rope_tables(pos_s, D_IDX, LANES)

    xp2 = x_prompt.reshape(t_p, d)
    xs2 = x_sample.reshape(t_s, d)

    def project(x2d, mod_, tabs, tm, rows_per_group, tiles_per_seq, idx_dtype):
        fox = _proj_fox(x2d, mod_[1], mod_[0], wqf, wkf, wvf, gqf, gkf, tm, rows_per_group)
        dsa = _proj_dsa(x2d, mod_[1], mod_[0], wqd, wkd, wvd, wqi.astype(idx_dtype), wmisc.astype(idx_dtype),
                        gqd, gkd, bf, tabs, tm, rows_per_group, tiles_per_seq)
        return fox, dsa

    (qf_p, kfb_p, vfb_p, kf_p, vf_p), (qd_p, kdb_p, vdb_p, qi_p, kia_p, kib_p, kd_p, vd_p, misc_p) = \
        project(xp2, mod_p, tabs_p, tm_p, s, s // tm_p, BF16)
    (qf_s, _, _, kf_s, vf_s), (qd_s, _, _, qi_s, _, _, kd_s, vd_s, misc_s) = \
        project(xs2, mod_s, tabs_s, tm_s, t_s, 1, F32)

    lf_p = misc_p[:, MISC_F:MISC_F + h_fox].reshape(b, s, h_fox)
    cum = _cumsum_lanes(lf_p.transpose(0, 2, 1).reshape(b * h_fox, s)).reshape(b, h_fox, s)
    o_fox_p = _fox_prompt(qf_p.reshape(b, s, nf), kfb_p.reshape(b, s, nf), vfb_p.reshape(b, s, nf), cum)
    o_dsa_p = _dsa_prompt(qi_p.reshape(b, s, ni), kia_p.reshape(b, s, LANES), kib_p.reshape(b, s, LANES),
                          misc_p.reshape(b, s, LANES), qd_p.reshape(b, s, nd),
                          kdb_p.reshape(b, s, nkv), vdb_p.reshape(b, s, nkv))

    lf_s = misc_s[:, MISC_F:MISC_F + h_fox]
    ck = cache_fox_k[layer].reshape(n_phys, PAGE_SIZE * h_fox, HEAD_DIM)
    cv = cache_fox_v[layer].reshape(n_phys, PAGE_SIZE * h_fox, HEAD_DIM)
    clf = cache_fox_logf[layer].astype(F32).reshape(n_phys, 1, PAGE_SIZE * h_fox)
    w_s = misc_s[:, MISC_W:MISC_W + H_IDX]
    n_pages = page_table.shape[1]
    o_fox_s, sc_past, sc_new = _fox_sample(
        page_table,
        qf_s.astype(F32).reshape(db, h_fox, HEAD_DIM),
        kf_s.reshape(db, h_fox, HEAD_DIM), vf_s.reshape(db, h_fox, HEAD_DIM),
        jnp.tile(lf_s, (1, PAGE_SIZE)).reshape(db, 1, PAGE_SIZE * h_fox),
        qi_s.astype(F32).reshape(db, H_IDX, D_IDX),
        jnp.broadcast_to(w_s[:, :, None], (db, H_IDX, LANES)),
        misc_s[:, MISC_KI:MISC_KI + D_IDX].reshape(db, 1, D_IDX),
        ck, cv, clf, cache_idx_k[layer].transpose(0, 2, 1))
    n_sel = min(TOPK_MAX, (past_len + dq) // 4)
    selb_past, selb_new = _select_sample(jnp.concatenate([sc_past, sc_new], axis=1), n_sel, n_pages)
    cdk = cache_dsa_k[layer].reshape(n_phys, PAGE_SIZE * KV_DSA, HEAD_DIM)
    cdv = cache_dsa_v[layer].reshape(n_phys, PAGE_SIZE * KV_DSA, HEAD_DIM)
    o_dsa_s = _dsa_sample(page_table, qd_s.astype(F32).reshape(db, h_dsa, HEAD_DIM),
                          kd_s.reshape(db, KV_DSA, HEAD_DIM), vd_s.reshape(db, KV_DSA, HEAD_DIM),
                          selb_past, selb_new, cdk, cdv)

    assert nf == nd
    wr = jnp.concatenate([w_router_group[layer],
                          w_router_expert[layer].transpose(1, 0, 2).reshape(d, N_EXPERTS),
                          jnp.zeros((d, LANES - N_GROUPS - N_EXPERTS), F32)], axis=1)
    x1_p, h2_p, route_p = _outproj(xp2, o_fox_p.reshape(t_p, nf), o_dsa_p.reshape(t_p, nd),
                                   w_out[layer].astype(BF16),
                                   mod_p[2], mod_p[4], mod_p[3], wr.astype(BF16), tm_p, s)
    x1_s, h2_s, route_s = _outproj(xs2, o_fox_s.reshape(t_s, nf), o_dsa_s.reshape(t_s, nd), w_out[layer],
                                   mod_s[2], mod_s[4], mod_s[3], wr, tm_s, t_s)

    expert_ids = jnp.concatenate([route_p[:, :2], route_s[:, :2]], axis=0).astype(I32)
    tile_expert, n_active, ends, slots = _moe_plan(expert_ids, tm_p)
    n_slots = tile_expert.shape[0] * tm_p
    x_sorted = _dispatch(ends, slots[:t_p], h2_p, slots[t_p:], h2_s, n_slots, tm_p)
    y_slots = _moe_experts(tile_expert, n_active, x_sorted, w_gate[layer], w_up[layer], w_down[layer], tm_p)
    y_p = _combine(slots[:t_p, 0], slots[:t_p, 1], x1_p, mod_p[5], route_p, y_slots, tm_p, s)
    y_s = _combine(slots[t_p:, 0], slots[t_p:, 1], x1_s, mod_s[5], route_s, y_slots, tm_s, t_s)

    def rows(kf, vf, misc, kd, vd, g, t):
        return (kf.reshape(1, g, t, h_fox, HEAD_DIM), vf.reshape(1, g, t, h_fox, HEAD_DIM),
                misc[:, MISC_F:MISC_F + h_fox].reshape(1, g, t, h_fox),
                kd.reshape(1, g, t, KV_DSA, HEAD_DIM), vd.reshape(1, g, t, KV_DSA, HEAD_DIM),
                misc[:, MISC_KI:MISC_KI + D_IDX].reshape(1, g, t, D_IDX))

    return ((y_p.reshape(b, s, d), y_s.reshape(db, dq, d))
            + rows(kf_p, vf_p, misc_p, kd_p, vd_p, b, s)
            + rows(kf_s, vf_s, misc_s, kd_s, vd_s, db, dq))
```
